```python
import math
import jax, jax.numpy as jnp
from jax import lax
import numpy as np

D_MODEL = 1024
BATCH = 16
SEQ = 2048
DEPTH = 4

CHUNK = 64
MIX_WIDTH = D_MODEL
CONV_WIDTH = MIX_WIDTH // 2
SSM_WIDTH = MIX_WIDTH - CONV_WIDTH
CONV_HEADS = 8
CONV_KERNEL = 31
SSM_GROUP = 16
SSM_GROUPS = SSM_WIDTH // SSM_GROUP
SSM_STATE = 64
DT_MIN = 1e-3
DT_MAX = 1e-1
IN_WIDTH = 2 * CONV_WIDTH + SSM_WIDTH
MOE_GROUPS = 4
EXPERTS_PER_GROUP = 8
N_EXPERTS = MOE_GROUPS * EXPERTS_PER_GROUP
TOP_K = 2
D_EXPERT = D_MODEL // 2
MOE_BLOCK = 128
EPS = 1e-6

kernel_name = "hybrid_conv_s5_hmoe_encoder"


def rms_norm(x, g):
    xf = x.astype(jnp.float32)
    y = xf * lax.rsqrt(jnp.mean(xf * xf, axis=-1, keepdims=True) + EPS)
    return (y * g.astype(jnp.float32)).astype(x.dtype)


def layer_norm(x, g, b):
    xf = x.astype(jnp.float32)
    mu = jnp.mean(xf, axis=-1, keepdims=True)
    xc = xf - mu
    y = xc * lax.rsqrt(jnp.mean(xc * xc, axis=-1, keepdims=True) + EPS)
    return (y * g.astype(jnp.float32) + b.astype(jnp.float32)).astype(x.dtype)


def conformer_conv(v, gate, w, b, ln_g, ln_b):
    u = v * jax.nn.sigmoid(gate)
    up = jnp.pad(u, ((0, 0), (CONV_KERNEL - 1, 0), (0, 0)))
    y = lax.conv_general_dilated(up, w[:, None, :], window_strides=(1,), padding='VALID',
                                 dimension_numbers=('NWC', 'WIO', 'NWC'),
                                 feature_group_count=CONV_WIDTH) + b
    y = layer_norm(y, ln_g, ln_b)
    return jax.nn.silu(y)


def _complex_scan_combine(left, right):
    a1r, a1i, b1r, b1i = left
    a2r, a2i, b2r, b2i = right
    return (a2r * a1r - a2i * a1i,
            a2r * a1i + a2i * a1r,
            a2r * b1r - a2i * b1i + b2r,
            a2r * b1i + a2i * b1r + b2i)


def s5_ssm(u, lam_re, lam_im, log_dt, b_re, b_im, c_re, c_im, d, glu_w, glu_b):
    f32 = jnp.float32
    bsz, seq, _ = u.shape
    uf = u.astype(f32)
    ug = uf.reshape(bsz, seq, SSM_GROUPS, SSM_GROUP)
    dt = jnp.exp(log_dt.astype(f32))[:, None]
    lr = lam_re.astype(f32)
    li = lam_im.astype(f32)
    mag = jnp.exp(lr * dt)
    ab_re = mag * jnp.cos(li * dt)
    ab_im = mag * jnp.sin(li * dt)
    den = lr * lr + li * li
    nr = ab_re - 1.0
    q_re = (nr * lr + ab_im * li) / den
    q_im = (ab_im * lr - nr * li) / den
    br = b_re.astype(f32)
    bi = b_im.astype(f32)
    bb_re = q_re[..., None] * br - q_im[..., None] * bi
    bb_im = q_re[..., None] * bi + q_im[..., None] * br
    drv_re = jnp.einsum('bsgc,gnc->sbgn', ug, bb_re)
    drv_im = jnp.einsum('bsgc,gnc->sbgn', ug, bb_im)
    a_re = jnp.broadcast_to(ab_re[None, None], (seq, 1, SSM_GROUPS, SSM_STATE))
    a_im = jnp.broadcast_to(ab_im[None, None], (seq, 1, SSM_GROUPS, SSM_STATE))
    _, _, h_re, h_im = lax.associative_scan(_complex_scan_combine,
                                            (a_re, a_im, drv_re, drv_im), axis=0)
    y = (jnp.einsum('sbgn,gcn->bsgc', h_re, c_re.astype(f32))
         - jnp.einsum('sbgn,gcn->bsgc', h_im, c_im.astype(f32)))
    y = y.reshape(bsz, seq, SSM_WIDTH) + d.astype(f32) * uf
    y = jax.nn.gelu(y).astype(u.dtype)
    return y * jax.nn.sigmoid(y @ glu_w + glu_b)


def hierarchical_moe(x, w_rg, b_rg, w_re, b_re, w_gate, w_up, w_down):
    f32 = jnp.float32
    bsz, seq, dm = x.shape
    n_tok = bsz * seq
    xt = x.reshape(n_tok, dm)
    xf = xt.astype(f32)
    tok_idx = jnp.arange(n_tok)
    g_logits = xf @ w_rg.astype(f32) + b_rg.astype(f32)
    g_prob = jax.nn.softmax(g_logits, axis=-1)
    grp = jnp.argmax(g_logits, axis=-1)
    p_grp = g_prob[tok_idx, grp]
    e_logits = (xf @ w_re.astype(f32) + b_re.astype(f32)).reshape(n_tok, MOE_GROUPS, EXPERTS_PER_GROUP)
    e_in = e_logits[tok_idx, grp]
    top_v, top_i = lax.top_k(e_in, TOP_K)
    gates = jax.nn.softmax(top_v, axis=-1) * p_grp[:, None]
    expert = (grp[:, None] * EXPERTS_PER_GROUP + top_i).reshape(-1).astype(jnp.int32)
    gate = gates.reshape(-1)
    token = jnp.repeat(tok_idx, TOP_K).astype(jnp.int32)
    n_asg = n_tok * TOP_K
    n_blocks = -(-n_asg // MOE_BLOCK) + N_EXPERTS
    cap = n_blocks * MOE_BLOCK
    order = jnp.argsort(expert)
    e_sorted = expert[order]
    counts = jnp.bincount(expert, length=N_EXPERTS)
    starts = jnp.cumsum(counts) - counts
    padded = (counts + MOE_BLOCK - 1) // MOE_BLOCK * MOE_BLOCK
    pad_ends = jnp.cumsum(padded)
    pad_starts = pad_ends - padded
    slot = pad_starts[e_sorted] + jnp.arange(n_asg) - starts[e_sorted]
    slot_token = jnp.full((cap,), n_tok, jnp.int32).at[slot].set(token[order])
    slot_gate = jnp.zeros((cap,), f32).at[slot].set(gate[order])
    block_expert = jnp.minimum(
        jnp.searchsorted(pad_ends, jnp.arange(n_blocks) * MOE_BLOCK, side='right'), N_EXPERTS - 1)
    x_pad = jnp.concatenate([xt, jnp.zeros((1, dm), xt.dtype)], axis=0)
    xb = x_pad[slot_token].reshape(n_blocks, MOE_BLOCK, dm)

    def expert_block(args):
        xblk, e = args
        hid = jax.nn.silu(xblk @ w_gate[e]) * (xblk @ w_up[e])
        return hid @ w_down[e]

    yb = lax.map(expert_block, (xb, block_expert)).reshape(cap, dm)
    y = jax.ops.segment_sum(yb.astype(f32) * slot_gate[:, None], slot_token, num_segments=n_tok + 1)[:n_tok]
    return y.astype(x.dtype).reshape(bsz, seq, dm)


def setup_inputs(seed: int = 0) -> dict:
    key = jax.random.key(seed)
    ks = jax.random.split(key, 32)
    f32 = jnp.float32
    L = DEPTH

    def nrm(k, shape, scale):
        return jax.random.normal(k, shape, f32) * scale

    x = jax.random.normal(ks[0], (BATCH, SEQ, D_MODEL), f32)
    g_mix = 1.0 + nrm(ks[1], (L, D_MODEL), 0.02)
    w_in = nrm(ks[2], (L, D_MODEL, IN_WIDTH), D_MODEL ** -0.5)
    conv_w = nrm(ks[3], (L, CONV_KERNEL, CONV_WIDTH), CONV_KERNEL ** -0.5)
    conv_b = nrm(ks[4], (L, CONV_WIDTH), 0.02)
    conv_ln_g = 1.0 + nrm(ks[5], (L, CONV_WIDTH), 0.02)
    conv_ln_b = nrm(ks[6], (L, CONV_WIDTH), 0.02)
    ssm_lam_re = -0.5 * jnp.exp(nrm(ks[7], (L, SSM_GROUPS, SSM_STATE), 0.05))
    ssm_lam_im = (jnp.pi * jnp.arange(SSM_STATE, dtype=f32))[None, None, :] + nrm(ks[8], (L, SSM_GROUPS, SSM_STATE), 0.01)
    ssm_log_dt = jax.random.uniform(ks[9], (L, SSM_GROUPS), f32, math.log(DT_MIN), math.log(DT_MAX))
    ssm_b_re = nrm(ks[10], (L, SSM_GROUPS, SSM_STATE, SSM_GROUP), (2 * SSM_GROUP) ** -0.5)
    ssm_b_im = nrm(ks[11], (L, SSM_GROUPS, SSM_STATE, SSM_GROUP), (2 * SSM_GROUP) ** -0.5)
    ssm_c_re = nrm(ks[12], (L, SSM_GROUPS, SSM_GROUP, SSM_STATE), (2 * SSM_STATE) ** -0.5)
    ssm_c_im = nrm(ks[13], (L, SSM_GROUPS, SSM_GROUP, SSM_STATE), (2 * SSM_STATE) ** -0.5)
    ssm_d = nrm(ks[14], (L, SSM_WIDTH), 1.0)
    ssm_glu_w = nrm(ks[15], (L, SSM_WIDTH, SSM_WIDTH), SSM_WIDTH ** -0.5)
    ssm_glu_b = nrm(ks[16], (L, SSM_WIDTH), 0.02)
    g_branch = 1.0 + nrm(ks[17], (L, MIX_WIDTH), 0.02)
    w_out = nrm(ks[18], (L, MIX_WIDTH, D_MODEL), MIX_WIDTH ** -0.5)
    g_ffn = 1.0 + nrm(ks[19], (L, D_MODEL), 0.02)
    w_router_group = nrm(ks[20], (L, D_MODEL, MOE_GROUPS), D_MODEL ** -0.5)
    b_router_group = nrm(ks[21], (L, MOE_GROUPS), 0.01)
    w_router_expert = nrm(ks[22], (L, D_MODEL, N_EXPERTS), D_MODEL ** -0.5)
    b_router_expert = nrm(ks[23], (L, N_EXPERTS), 0.01)
    w_gate = nrm(ks[24], (L, N_EXPERTS, D_MODEL, D_EXPERT), D_MODEL ** -0.5)
    w_up = nrm(ks[25], (L, N_EXPERTS, D_MODEL, D_EXPERT), D_MODEL ** -0.5)
    w_down = nrm(ks[26], (L, N_EXPERTS, D_EXPERT, D_MODEL), D_EXPERT ** -0.5)
    g_final = 1.0 + nrm(ks[27], (D_MODEL,), 0.02)
    return {"x": x, "g_mix": g_mix, "w_in": w_in, "conv_w": conv_w, "conv_b": conv_b,
            "conv_ln_g": conv_ln_g, "conv_ln_b": conv_ln_b,
            "ssm_lam_re": ssm_lam_re, "ssm_lam_im": ssm_lam_im, "ssm_log_dt": ssm_log_dt,
            "ssm_b_re": ssm_b_re, "ssm_b_im": ssm_b_im, "ssm_c_re": ssm_c_re, "ssm_c_im": ssm_c_im,
            "ssm_d": ssm_d, "ssm_glu_w": ssm_glu_w, "ssm_glu_b": ssm_glu_b,
            "g_branch": g_branch, "w_out": w_out, "g_ffn": g_ffn,
            "w_router_group": w_router_group, "b_router_group": b_router_group,
            "w_router_expert": w_router_expert, "b_router_expert": b_router_expert,
            "w_gate": w_gate, "w_up": w_up, "w_down": w_down, "g_final": g_final}


def reference(x, g_mix, w_in, conv_w, conv_b, conv_ln_g, conv_ln_b,
              ssm_lam_re, ssm_lam_im, ssm_log_dt, ssm_b_re, ssm_b_im, ssm_c_re, ssm_c_im,
              ssm_d, ssm_glu_w, ssm_glu_b, g_branch, w_out, g_ffn,
              w_router_group, b_router_group, w_router_expert, b_router_expert,
              w_gate, w_up, w_down, g_final):
    for l in range(DEPTH):
        h = rms_norm(x, g_mix[l])
        z = h @ w_in[l]
        c_val = z[..., :CONV_WIDTH]
        c_gate = z[..., CONV_WIDTH:2 * CONV_WIDTH]
        s_in = z[..., 2 * CONV_WIDTH:]
        y_conv = conformer_conv(c_val, c_gate, conv_w[l], conv_b[l], conv_ln_g[l], conv_ln_b[l])
        y_ssm = s5_ssm(s_in, ssm_lam_re[l], ssm_lam_im[l], ssm_log_dt[l], ssm_b_re[l], ssm_b_im[l],
                       ssm_c_re[l], ssm_c_im[l], ssm_d[l], ssm_glu_w[l], ssm_glu_b[l])
        gb = g_branch[l]
        y_cat = jnp.concatenate([rms_norm(y_conv, gb[:CONV_WIDTH]),
                                 rms_norm(y_ssm, gb[CONV_WIDTH:])], axis=-1)
        x = x + y_cat @ w_out[l]
        h = rms_norm(x, g_ffn[l])
        x = x + hierarchical_moe(h, w_router_group[l], b_router_group[l],
                                 w_router_expert[l], b_router_expert[l],
                                 w_gate[l], w_up[l], w_down[l])
    return rms_norm(x, g_final)
```

```python
import functools

import jax
import jax.numpy as jnp
from jax import lax
from jax.experimental import pallas as pl
from jax.experimental.pallas import tpu as pltpu

F32 = jnp.float32
BF16 = jnp.bfloat16

EPS = 1e-6
CONV_KERNEL = 31
SSM_GROUP = 16
SSM_STATE = 64
MOE_GROUPS = 4
EXPERTS_PER_GROUP = 8
N_EXPERTS = MOE_GROUPS * EXPERTS_PER_GROUP
LANES = 128
SUBLANES = 8
MXU_COLS = 256
VMEM_LIMIT = 56 * 1024 * 1024

ROW_TILE = 512
HALO_STEPS = 32
MOE_BLOCK = 256


def _params(sem):
    return pltpu.CompilerParams(dimension_semantics=sem, vmem_limit_bytes=VMEM_LIMIT)


def _rms(x, g):
    return x * lax.rsqrt(jnp.mean(x * x, axis=-1, keepdims=True) + EPS) * g


def _copy_kernel(x_ref, o_ref):
    o_ref[...] = x_ref[...]


def _to_time_major(x):
    bsz, seq, dm = x.shape
    tt = min(seq, 256)
    out = pl.pallas_call(
        _copy_kernel,
        grid=(seq // tt, bsz),
        in_specs=[pl.BlockSpec((None, tt, dm), lambda i, b: (b, i, 0))],
        out_specs=pl.BlockSpec((tt, dm), lambda i, b: (i, b)),
        out_shape=jax.ShapeDtypeStruct((seq, bsz * dm), x.dtype),
        compiler_params=_params(("parallel", "parallel")),
        name="to_time_major",
    )(x)
    return out.reshape(seq * bsz, dm)


def _combine(x_ref, y0_ref, y1_ref, meta_ref):
    meta = meta_ref[...]
    return x_ref[...] + meta[:, 2:3] * y0_ref[...] + meta[:, 3:4] * y1_ref[...]


def _inproj_body(x, g_ref, w_ref, u_ref, s_ref):
    cw = u_ref.shape[-1]
    h = _rms(x, g_ref[...]).astype(BF16)
    z = jnp.dot(h, w_ref[...], preferred_element_type=F32)
    u_ref[...] = z[:, :cw] * jax.nn.sigmoid(z[:, cw:2 * cw])
    s_ref[...] = z[:, 2 * cw:]


def _inproj_kernel(x_ref, g_ref, w_ref, u_ref, s_ref):
    _inproj_body(x_ref[...], g_ref, w_ref, u_ref, s_ref)


def _inproj_combine_kernel(x_ref, y0_ref, y1_ref, meta_ref, g_ref, w_ref, xo_ref, u_ref, s_ref):
    x = _combine(x_ref, y0_ref, y1_ref, meta_ref)
    xo_ref[...] = x
    _inproj_body(x, g_ref, w_ref, u_ref, s_ref)


def _in_projection(x, moe, g, w_bf, conv_width):
    n, dm = x.shape
    tm = min(ROW_TILE, n)
    in_w = w_bf.shape[1]
    sw = in_w - 2 * conv_width
    row = lambda i: (i, 0)
    fixed = lambda i: (0, 0)
    wspecs = [pl.BlockSpec((1, dm), fixed), pl.BlockSpec((dm, in_w), fixed)]
    outs = [jax.ShapeDtypeStruct((n, conv_width), F32), jax.ShapeDtypeStruct((n, sw), F32)]
    ospecs = [pl.BlockSpec((tm, conv_width), row), pl.BlockSpec((tm, sw), row)]
    if moe is None:
        u, s = pl.pallas_call(
            _inproj_kernel, grid=(n // tm,),
            in_specs=[pl.BlockSpec((tm, dm), row)] + wspecs,
            out_specs=ospecs, out_shape=outs,
            compiler_params=_params(("parallel",)), name="in_projection",
        )(x, g, w_bf)
        return x, u, s
    y_all, meta = moe
    nb = n // tm
    xo, u, s = pl.pallas_call(
        _inproj_combine_kernel, grid=(nb,),
        in_specs=[pl.BlockSpec((tm, dm), row),
                  pl.BlockSpec((tm, dm), row),
                  pl.BlockSpec((tm, dm), lambda i: (i + nb, 0)),
                  pl.BlockSpec((tm, LANES), row)] + wspecs,
        out_specs=[pl.BlockSpec((tm, dm), row)] + ospecs,
        out_shape=[jax.ShapeDtypeStruct((n, dm), F32)] + outs,
        compiler_params=_params(("parallel",)), name="combine_in_projection",
    )(x, y_all, y_all, meta, g, w_bf)
    return xo, u, s


CONV_CHUNK = 32


def _conv_kernel(halo_ref, u_ref, w_ref, b_ref, lng_ref, lnb_ref, gb_ref, o_ref, buf_ref, *, bsz):
    rows = u_ref.shape[0]
    halo = halo_ref.shape[0]
    i = pl.program_id(0)

    @pl.when(i == 0)
    def _():
        buf_ref[0:halo, :] = jnp.zeros(halo_ref.shape, F32)

    @pl.when(i > 0)
    def _():
        buf_ref[0:halo, :] = halo_ref[...]

    buf_ref[halo:, :] = u_ref[...]

    def chunk(c, carry):
        r0 = pl.multiple_of(c * CONV_CHUNK, CONV_CHUNK)
        acc = jnp.zeros((CONV_CHUNK, u_ref.shape[1]), F32)
        for k in range(CONV_KERNEL):
            off = halo - (CONV_KERNEL - 1 - k) * bsz
            acc = acc + buf_ref[pl.ds(r0 + off, CONV_CHUNK), :] * w_ref[k:k + 1, :]
        y = acc + b_ref[...]
        mu = jnp.mean(y, axis=-1, keepdims=True)
        yc = y - mu
        yn = yc * lax.rsqrt(jnp.mean(yc * yc, axis=-1, keepdims=True) + EPS)
        yn = yn * lng_ref[...] + lnb_ref[...]
        act = yn * jax.nn.sigmoid(yn)
        o_ref[pl.ds(r0, CONV_CHUNK), :] = _rms(act, gb_ref[...])
        return carry

    lax.fori_loop(0, rows // CONV_CHUNK, chunk, 0)


def _conv_branch(u, w, b, ln_g, ln_b, gb, bsz):
    n, cw = u.shape
    halo = HALO_STEPS * bsz
    rows = min(ROW_TILE, n)
    assert rows % halo == 0 and halo >= (CONV_KERNEL - 1) * bsz
    ratio = rows // halo
    fixed = lambda i: (0, 0)
    return pl.pallas_call(
        functools.partial(_conv_kernel, bsz=bsz),
        grid=(n // rows,),
        in_specs=[pl.BlockSpec((halo, cw), lambda i: (jnp.maximum(i * ratio - 1, 0), 0)),
                  pl.BlockSpec((rows, cw), lambda i: (i, 0)),
                  pl.BlockSpec((CONV_KERNEL, cw), fixed),
                  pl.BlockSpec((1, cw), fixed), pl.BlockSpec((1, cw), fixed),
                  pl.BlockSpec((1, cw), fixed), pl.BlockSpec((1, cw), fixed)],
        out_specs=pl.BlockSpec((rows, cw), lambda i: (i, 0)),
        out_shape=jax.ShapeDtypeStruct((n, cw), F32),
        scratch_shapes=[pltpu.VMEM((halo + rows, cw), F32)],
        compiler_params=_params(("parallel",)), name="conv_branch",
    )(u, u, w, b, ln_g, ln_b, gb)


SCAN_TILES = 4


def _ssm_kernel(s_ref, wd_ref, are_ref, aim_ref, wc_ref, dvec_ref, gw_ref, gbias_ref, gb_ref,
                o_ref, d_ref, h_ref, *, bsz):
    rows, sw = s_ref.shape
    n_state = d_ref.shape[1]
    n_tiles = n_state // MXU_COLS
    tiles_per_lane_block = LANES // (2 * SSM_GROUP)
    halves = bsz // SUBLANES
    steps = rows // bsz

    @pl.when(pl.program_id(0) == 0)
    def _():
        h_ref[...] = jnp.zeros(h_ref.shape, F32)

    s = s_ref[...]
    s_bf = s.astype(BF16)
    for j in range(n_tiles):
        q = j // tiles_per_lane_block
        d_ref[:, j * MXU_COLS:(j + 1) * MXU_COLS] = jnp.dot(
            s_bf[:, q * LANES:(q + 1) * LANES], wd_ref[j], preferred_element_type=F32)

    for jq in range(n_tiles // SCAN_TILES):
        cols = [(jq * SCAN_TILES + jj) * MXU_COLS for jj in range(SCAN_TILES)]
        a_re = [jnp.broadcast_to(are_ref[:, c // 2:c // 2 + LANES], (SUBLANES, LANES)) for c in cols]
        a_im = [jnp.broadcast_to(aim_ref[:, c // 2:c // 2 + LANES], (SUBLANES, LANES)) for c in cols]
        init = []
        for c in cols:
            for hf in range(halves):
                r = hf * SUBLANES
                init.append(h_ref[r:r + SUBLANES, c:c + LANES])
                init.append(h_ref[r:r + SUBLANES, c + LANES:c + 2 * LANES])

        def step(t, carry, cols=cols, a_re=a_re, a_im=a_im):
            new = []
            k = 0
            for jj, c in enumerate(cols):
                for hf in range(halves):
                    r0 = pl.multiple_of(t * bsz + hf * SUBLANES, SUBLANES)
                    hr, hi = carry[k], carry[k + 1]
                    k += 2
                    dr = d_ref[pl.ds(r0, SUBLANES), c:c + LANES]
                    di = d_ref[pl.ds(r0, SUBLANES), c + LANES:c + 2 * LANES]
                    nr = a_re[jj] * hr - a_im[jj] * hi + dr
                    ni = a_re[jj] * hi + a_im[jj] * hr + di
                    d_ref[pl.ds(r0, SUBLANES), c:c + LANES] = nr
                    d_ref[pl.ds(r0, SUBLANES), c + LANES:c + 2 * LANES] = ni
                    new += [nr, ni]
            return tuple(new)

        fin = lax.fori_loop(0, steps, step, tuple(init))
        k = 0
        for c in cols:
            for hf in range(halves):
                r = hf * SUBLANES
                h_ref[r:r + SUBLANES, c:c + LANES] = fin[k]
                h_ref[r:r + SUBLANES, c + LANES:c + 2 * LANES] = fin[k + 1]
                k += 2

    n_out = sw // LANES
    kw = n_state // n_out
    ys = []
    for q in range(n_out):
        hq = d_ref[:, q * kw:(q + 1) * kw].astype(BF16)
        ys.append(jnp.dot(hq, wc_ref[q], preferred_element_type=F32))
    y = jnp.concatenate(ys, axis=-1) + dvec_ref[...] * s
    act = jax.nn.gelu(y)
    z = jnp.dot(act.astype(BF16), gw_ref[...], preferred_element_type=F32) + gbias_ref[...]
    o_ref[...] = _rms(act * jax.nn.sigmoid(z), gb_ref[...])


def _ssm_weights(lam_re, lam_im, log_dt, b_re, b_im, c_re, c_im):
    g, n = lam_re.shape
    dt = jnp.exp(log_dt)[:, None]
    mag = jnp.exp(lam_re * dt)
    ab_re = mag * jnp.cos(lam_im * dt)
    ab_im = mag * jnp.sin(lam_im * dt)
    den = lam_re * lam_re + lam_im * lam_im
    nr = ab_re - 1.0
    q_re = (nr * lam_re + ab_im * lam_im) / den
    q_im = (ab_im * lam_re - nr * lam_im) / den
    bb_re = q_re[..., None] * b_re - q_im[..., None] * b_im
    bb_im = q_re[..., None] * b_im + q_im[..., None] * b_re
    sel = jnp.eye(g, dtype=F32).reshape(g, g // 2, 2)
    bbp = jnp.stack([bb_re, bb_im])
    wd = jnp.einsum('gji,pgnc->gcjpin', sel, bbp).reshape(g * SSM_GROUP, 2 * g * n)
    ccp = jnp.stack([c_re, -c_im])
    wc = jnp.einsum('gji,pgcn->jpingc', sel, ccp).reshape(2 * g * n, g * SSM_GROUP)
    n_tiles = 2 * g * n // MXU_COLS
    per_block = LANES // (2 * SSM_GROUP)
    wd_t = jnp.stack([wd[(j // per_block) * LANES:(j // per_block + 1) * LANES,
                         j * MXU_COLS:(j + 1) * MXU_COLS] for j in range(n_tiles)])
    n_out = g * SSM_GROUP // LANES
    kw = 2 * g * n // n_out
    wc_t = jnp.stack([wc[q * kw:(q + 1) * kw, q * LANES:(q + 1) * LANES] for q in range(n_out)])
    a_re = ab_re.reshape(1, g * n)
    a_im = ab_im.reshape(1, g * n)
    return wd_t.astype(BF16), a_re, a_im, wc_t.astype(BF16)


def _ssm_branch(s_in, wd, a_re, a_im, wc, dvec, glu_w_bf, glu_b, gb, bsz):
    n, sw = s_in.shape
    rows = min(ROW_TILE, n)
    n_state = 2 * a_re.shape[1]
    fixed2 = lambda i: (0, 0)
    fixed3 = lambda i: (0, 0, 0)
    vec = pl.BlockSpec((1, sw), fixed2)
    return pl.pallas_call(
        functools.partial(_ssm_kernel, bsz=bsz),
        grid=(n // rows,),
        in_specs=[pl.BlockSpec((rows, sw), lambda i: (i, 0)),
                  pl.BlockSpec(wd.shape, fixed3),
                  pl.BlockSpec(a_re.shape, fixed2), pl.BlockSpec(a_im.shape, fixed2),
                  pl.BlockSpec(wc.shape, fixed3),
                  vec, pl.BlockSpec((sw, sw), fixed2), vec, vec],
        out_specs=pl.BlockSpec((rows, sw), lambda i: (i, 0)),
        out_shape=jax.ShapeDtypeStruct((n, sw), F32),
        scratch_shapes=[pltpu.VMEM((rows, n_state), F32), pltpu.VMEM((bsz, n_state), F32)],
        compiler_params=_params(("arbitrary",)), name="ssm_branch",
    )(s_in, wd, a_re, a_im, wc, dvec, glu_w_bf, glu_b, gb)


def _outproj_router_kernel(x_ref, yc_ref, ys_ref, wo_ref, g_ref, wr_hi_ref, wr_lo_ref, br_ref,
                           xo_ref, h_ref, meta_ref):
    ycat = jnp.concatenate([yc_ref[...].astype(BF16), ys_ref[...].astype(BF16)], axis=-1)
    x = x_ref[...] + jnp.dot(ycat, wo_ref[...], preferred_element_type=F32)
    xo_ref[...] = x
    h = _rms(x, g_ref[...])
    h_ref[...] = h
    h_hi = h.astype(BF16)
    h_lo = (h - h_hi.astype(F32)).astype(BF16)
    w_hi = wr_hi_ref[...]
    lg = (jnp.dot(h_hi, w_hi, preferred_element_type=F32)
          + jnp.dot(h_lo, w_hi, preferred_element_type=F32)
          + jnp.dot(h_hi, wr_lo_ref[...], preferred_element_type=F32)) + br_ref[...]
    lane = lax.broadcasted_iota(jnp.int32, lg.shape, 1).astype(F32)
    ninf = jnp.float32(-jnp.inf)
    big = jnp.float32(LANES)
    is_g = lane < MOE_GROUPS
    gl = jnp.where(is_g, lg, ninf)
    gmax = jnp.max(gl, axis=-1, keepdims=True)
    grp = jnp.min(jnp.where(gl == gmax, lane, big), axis=-1, keepdims=True)
    den = jnp.sum(jnp.where(is_g, jnp.exp(gl - gmax), 0.0), axis=-1, keepdims=True)
    p_grp = 1.0 / den
    lo = MOE_GROUPS + EXPERTS_PER_GROUP * grp
    el = jnp.where((lane >= lo) & (lane < lo + EXPERTS_PER_GROUP), lg, ninf)
    v1 = jnp.max(el, axis=-1, keepdims=True)
    i1 = jnp.min(jnp.where(el == v1, lane, big), axis=-1, keepdims=True)
    el2 = jnp.where(lane == i1, ninf, el)
    v2 = jnp.max(el2, axis=-1, keepdims=True)
    i2 = jnp.min(jnp.where(el2 == v2, lane, big), axis=-1, keepdims=True)
    t = jnp.exp(v2 - v1)
    g1 = p_grp / (1.0 + t)
    g2 = p_grp * t / (1.0 + t)
    meta = jnp.where(lane == 0, i1 - MOE_GROUPS,
                     jnp.where(lane == 1, i2 - MOE_GROUPS,
                               jnp.where(lane == 2, g1, jnp.where(lane == 3, g2, 0.0))))
    meta_ref[...] = meta


def _outproj_router(x, yc, ys, wo_bf, g, wr_hi, wr_lo, br):
    n, dm = x.shape
    cw = yc.shape[1]
    sw = ys.shape[1]
    tm = min(ROW_TILE, n)
    row = lambda i: (i, 0)
    fixed = lambda i: (0, 0)
    return pl.pallas_call(
        _outproj_router_kernel, grid=(n // tm,),
        in_specs=[pl.BlockSpec((tm, dm), row), pl.BlockSpec((tm, cw), row), pl.BlockSpec((tm, sw), row),
                  pl.BlockSpec((cw + sw, dm), fixed), pl.BlockSpec((1, dm), fixed),
                  pl.BlockSpec((dm, LANES), fixed), pl.BlockSpec((dm, LANES), fixed),
                  pl.BlockSpec((1, LANES), fixed)],
        out_specs=[pl.BlockSpec((tm, dm), row), pl.BlockSpec((tm, dm), row), pl.BlockSpec((tm, LANES), row)],
        out_shape=[jax.ShapeDtypeStruct((n, dm), F32), jax.ShapeDtypeStruct((n, dm), F32),
                   jax.ShapeDtypeStruct((n, LANES), F32)],
        compiler_params=_params(("parallel",)), name="out_projection_router",
    )(x, yc, ys, wo_bf, g, wr_hi, wr_lo, br)


def _expert_kernel(be_ref, nvalid_ref, idx_hbm, h_hbm, wgu_ref, wd_ref, out_hbm,
                   idx_smem, xbuf, ybuf, isem, gsem, ssem):
    j = pl.program_id(0)
    blk = xbuf.shape[0]
    de = wd_ref.shape[0]
    n_valid = nvalid_ref[j]

    @pl.when(n_valid > 0)
    def _():
        icp = pltpu.make_async_copy(idx_hbm.at[pl.ds(j, 1)], idx_smem, isem)
        icp.start()
        icp.wait()

        def gather(r, carry):
            tok = idx_smem[0, r]
            pltpu.make_async_copy(h_hbm.at[pl.ds(tok, 1)], xbuf.at[pl.ds(r, 1)], gsem).start()
            return carry

        lax.fori_loop(0, blk, gather, 0)
        pltpu.make_async_copy(h_hbm.at[pl.ds(0, blk)], xbuf, gsem).wait()

        x = xbuf[...].astype(BF16)
        gu = jnp.dot(x, wgu_ref[...], preferred_element_type=F32)
        gate = gu[:, :de]
        hid = gate * jax.nn.sigmoid(gate) * gu[:, de:]
        ybuf[...] = jnp.dot(hid.astype(BF16), wd_ref[...], preferred_element_type=F32)

        def scatter(r, carry):
            dst = idx_smem[0, blk + r]
            pltpu.make_async_copy(ybuf.at[pl.ds(r, 1)], out_hbm.at[pl.ds(dst, 1)], ssem).start()
            return carry

        lax.fori_loop(0, n_valid, scatter, 0)

        @pl.when(n_valid == blk)
        def _():
            pltpu.make_async_copy(ybuf, out_hbm.at[pl.ds(0, blk)], ssem).wait()

        @pl.when(n_valid < blk)
        def _():
            def wait_row(r, carry):
                pltpu.make_async_copy(ybuf.at[pl.ds(0, 1)], out_hbm.at[pl.ds(0, 1)], ssem).wait()
                return carry

            lax.fori_loop(0, n_valid, wait_row, 0)


def _experts(h, meta, wgu_bf, wd_bf):
    n, dm = h.shape
    n_exp, _, de2 = wgu_bf.shape
    blk = MOE_BLOCK
    n_asg = 2 * n
    n_blocks = -(-n_asg // blk) + n_exp
    cap = n_blocks * blk
    expert = meta[:, :2].astype(jnp.int32).reshape(-1)
    onehot = (expert[:, None] == jnp.arange(n_exp, dtype=jnp.int32)[None, :]).astype(jnp.int32)
    csum = jnp.cumsum(onehot, axis=0)
    rank = jnp.take_along_axis(csum, expert[:, None], axis=1)[:, 0] - 1
    counts = csum[-1]
    padded = (counts + blk - 1) // blk * blk
    pad_ends = jnp.cumsum(padded)
    pad_starts = pad_ends - padded
    slot = pad_starts[expert] + rank
    asg = jnp.arange(n_asg, dtype=jnp.int32)
    token = asg // 2
    dst = (asg % 2) * n + token
    slot_src = jnp.zeros((cap,), jnp.int32).at[slot].set(token)
    slot_dst = jnp.zeros((cap,), jnp.int32).at[slot].set(dst)
    idx = jnp.concatenate([slot_src.reshape(n_blocks, blk), slot_dst.reshape(n_blocks, blk)], axis=1)
    n_used = pad_ends[-1] // blk
    block_start = jnp.minimum(jnp.arange(n_blocks, dtype=jnp.int32), n_used - 1) * blk
    block_expert = jnp.minimum(jnp.searchsorted(pad_ends, block_start, side='right'),
                               n_exp - 1).astype(jnp.int32)
    real_end = (pad_starts + counts)[block_expert]
    block_valid = jnp.clip(real_end - jnp.arange(n_blocks, dtype=jnp.int32) * blk, 0, blk).astype(jnp.int32)

    grid_spec = pltpu.PrefetchScalarGridSpec(
        num_scalar_prefetch=2,
        grid=(n_blocks,),
        in_specs=[pl.BlockSpec(memory_space=pl.ANY),
                  pl.BlockSpec(memory_space=pl.ANY),
                  pl.BlockSpec((None, dm, de2), lambda j, be, nu: (be[j], 0, 0)),
                  pl.BlockSpec((None, de2 // 2, dm), lambda j, be, nu: (be[j], 0, 0))],
        out_specs=pl.BlockSpec(memory_space=pl.ANY),
        scratch_shapes=[pltpu.SMEM((1, 2 * blk), jnp.int32),
                        pltpu.VMEM((blk, dm), F32),
                        pltpu.VMEM((blk, dm), F32),
                        pltpu.SemaphoreType.DMA, pltpu.SemaphoreType.DMA, pltpu.SemaphoreType.DMA],
    )
    return pl.pallas_call(
        _expert_kernel, grid_spec=grid_spec,
        out_shape=jax.ShapeDtypeStruct((n_asg, dm), F32),
        compiler_params=pltpu.CompilerParams(dimension_semantics=("arbitrary",),
                                             vmem_limit_bytes=VMEM_LIMIT, disable_bounds_checks=True),
        name="expert_blocks",
    )(block_expert, block_valid, idx, h, wgu_bf, wd_bf)


def _final_kernel(x_ref, y0_ref, y1_ref, meta_ref, g_ref, o_ref):
    o_ref[...] = _rms(_combine(x_ref, y0_ref, y1_ref, meta_ref), g_ref[...])


def _final_norm(x, y_all, meta, g, bsz):
    n, dm = x.shape
    seq = n // bsz
    tt = min(seq, 256)
    nb = seq // tt
    xv = x.reshape(seq, bsz * dm)
    yv = y_all.reshape(y_all.shape[0] // bsz, bsz * dm)
    mv = meta.reshape(seq, bsz * LANES)
    return pl.pallas_call(
        _final_kernel, grid=(nb, bsz),
        in_specs=[pl.BlockSpec((tt, dm), lambda i, b: (i, b)),
                  pl.BlockSpec((tt, dm), lambda i, b: (i, b)),
                  pl.BlockSpec((tt, dm), lambda i, b: (i + nb, b)),
                  pl.BlockSpec((tt, LANES), lambda i, b: (i, b)),
                  pl.BlockSpec((1, dm), lambda i, b: (0, 0))],
        out_specs=pl.BlockSpec((None, tt, dm), lambda i, b: (b, i, 0)),
        out_shape=jax.ShapeDtypeStruct((bsz, seq, dm), F32),
        compiler_params=_params(("parallel", "parallel")), name="final_norm",
    )(xv, yv, yv, mv, g)


def kernel(x, g_mix, w_in, conv_w, conv_b, conv_ln_g, conv_ln_b, ssm_lam_re, ssm_lam_im, ssm_log_dt,
           ssm_b_re, ssm_b_im, ssm_c_re, ssm_c_im, ssm_d, ssm_glu_w, ssm_glu_b, g_branch, w_out, g_ffn,
           w_router_group, b_router_group, w_router_expert, b_router_expert, w_gate, w_up, w_down, g_final):
    bsz, seq, dm = x.shape
    depth = w_in.shape[0]
    cw = conv_w.shape[-1]
    assert bsz % SUBLANES == 0
    row2 = lambda v: v.reshape(1, -1)

    xt = _to_time_major(x)
    moe = None
    for l in range(depth):
        xt, u, s_in = _in_projection(xt, moe, row2(g_mix[l]), w_in[l].astype(BF16), cw)
        gb = g_branch[l]
        yc = _conv_branch(u, conv_w[l], row2(conv_b[l]), row2(conv_ln_g[l]), row2(conv_ln_b[l]),
                          row2(gb[:cw]), bsz)
        wd, a_re, a_im, wc = _ssm_weights(ssm_lam_re[l], ssm_lam_im[l], ssm_log_dt[l],
                                          ssm_b_re[l], ssm_b_im[l], ssm_c_re[l], ssm_c_im[l])
        ys = _ssm_branch(s_in, wd, a_re, a_im, wc, row2(ssm_d[l]), ssm_glu_w[l].astype(BF16),
                         row2(ssm_glu_b[l]), row2(gb[cw:]), bsz)
        w_r = jnp.concatenate([w_router_group[l], w_router_expert[l]], axis=1)
        w_r = jnp.pad(w_r, ((0, 0), (0, LANES - w_r.shape[1])))
        wr_hi = w_r.astype(BF16)
        wr_lo = (w_r - wr_hi.astype(F32)).astype(BF16)
        b_r = jnp.concatenate([b_router_group[l], b_router_expert[l]])
        b_r = jnp.pad(b_r, (0, LANES - b_r.shape[0])).reshape(1, LANES)
        xt, h, meta = _outproj_router(xt, yc, ys, w_out[l].astype(BF16), row2(g_ffn[l]), wr_hi, wr_lo, b_r)
        wgu = jnp.concatenate([w_gate[l], w_up[l]], axis=-1).astype(BF16)
        y_all = _experts(h, meta, wgu, w_down[l].astype(BF16))
        moe = (y_all, meta)
    return _final_norm(xt, moe[0], moe[1], row2(g_final), bsz)
```

```python
import functools

import jax
import jax.numpy as jnp
from jax import lax
from jax.experimental import pallas as pl
from jax.experimental.pallas import tpu as pltpu

F32 = jnp.float32
BF16 = jnp.bfloat16
I32 = jnp.int32

EPS = 1e-6
CONV_KERNEL = 31
SSM_GROUP = 16
SSM_STATE = 64
MOE_GROUPS = 4
EXPERTS_PER_GROUP = 8
N_EXPERTS = MOE_GROUPS * EXPERTS_PER_GROUP
LANES = 128
SUBLANES = 8
MXU_COLS = 256
VMEM_LIMIT = 56 * 1024 * 1024

ROW_TILE = 512
HALO_STEPS = 32
MOE_BLOCK = 256
LAYOUT_STEPS = 256


def _params(sem, **kw):
    return pltpu.CompilerParams(dimension_semantics=sem, vmem_limit_bytes=VMEM_LIMIT, **kw)


def _rms(x, g):
    return x * lax.rsqrt(jnp.mean(x * x, axis=-1, keepdims=True) + EPS) * g


def _to_tm_kernel(x_ref, o_ref):
    bsz, tt, _ = x_ref.shape
    for b in range(bsz):
        o_ref[pl.ds(b, tt, stride=bsz), :] = x_ref[b]


def _from_tm_kernel(x_ref, o_ref):
    bsz, tt, _ = o_ref.shape
    for b in range(bsz):
        o_ref[b] = x_ref[pl.ds(b, tt, stride=bsz), :]


def _to_time_major(x):
    bsz, seq, dm = x.shape
    tt = min(seq, LAYOUT_STEPS)
    return pl.pallas_call(
        _to_tm_kernel, grid=(seq // tt, dm // LANES),
        in_specs=[pl.BlockSpec((bsz, tt, LANES), lambda i, c: (0, i, c))],
        out_specs=pl.BlockSpec((tt * bsz, LANES), lambda i, c: (i, c)),
        out_shape=jax.ShapeDtypeStruct((seq * bsz, dm), x.dtype),
        compiler_params=_params(("parallel", "parallel")), name="to_time_major",
    )(x)


def _from_time_major(xt, bsz):
    n, dm = xt.shape
    seq = n // bsz
    tt = min(seq, LAYOUT_STEPS)
    return pl.pallas_call(
        _from_tm_kernel, grid=(seq // tt, dm // LANES),
        in_specs=[pl.BlockSpec((tt * bsz, LANES), lambda i, c: (i, c))],
        out_specs=pl.BlockSpec((bsz, tt, LANES), lambda i, c: (0, i, c)),
        out_shape=jax.ShapeDtypeStruct((bsz, seq, dm), xt.dtype),
        compiler_params=_params(("parallel", "parallel")), name="from_time_major",
    )(xt)


def _inproj_body(x, g_ref, w_ref, u_ref, s_ref):
    cw = u_ref.shape[-1]
    h = _rms(x, g_ref[...]).astype(BF16)
    z = jnp.dot(h, w_ref[...], preferred_element_type=F32)
    u_ref[...] = z[:, :cw] * jax.nn.sigmoid(z[:, cw:2 * cw])
    s_ref[...] = z[:, 2 * cw:]


def _inproj_kernel(x_ref, g_ref, w_ref, u_ref, s_ref):
    _inproj_body(x_ref[...], g_ref, w_ref, u_ref, s_ref)


def _in_projection(x, g, w_bf, conv_width):
    n, dm = x.shape
    tm = min(ROW_TILE, n)
    in_w = w_bf.shape[1]
    sw = in_w - 2 * conv_width
    row = lambda i: (i, 0)
    fixed = lambda i: (0, 0)
    return pl.pallas_call(
        _inproj_kernel, grid=(n // tm,),
        in_specs=[pl.BlockSpec((tm, dm), row), pl.BlockSpec((1, dm), fixed), pl.BlockSpec((dm, in_w), fixed)],
        out_specs=[pl.BlockSpec((tm, conv_width), row), pl.BlockSpec((tm, sw), row)],
        out_shape=[jax.ShapeDtypeStruct((n, conv_width), F32), jax.ShapeDtypeStruct((n, sw), F32)],
        compiler_params=_params(("parallel",)), name="in_projection",
    )(x, g, w_bf)


def _gather_expert_rows(i, n_steps, tm, slots_hbm, ys_hbm, sl_smem, ybuf, csem, gsem):
    def fetch(tile, b):
        cp = pltpu.make_async_copy(slots_hbm.at[tile], sl_smem.at[b], csem)
        cp.start()
        cp.wait()

        def issue(r, carry):
            s0 = sl_smem[b, 0, r]
            s1 = sl_smem[b, 1, r]
            pltpu.make_async_copy(ys_hbm.at[pl.ds(s0, 1)], ybuf.at[b].at[pl.ds(r, 1)], gsem.at[b]).start()
            pltpu.make_async_copy(ys_hbm.at[pl.ds(s1, 1)], ybuf.at[b].at[pl.ds(tm + r, 1)], gsem.at[b]).start()
            return carry

        lax.fori_loop(0, tm, issue, 0, unroll=8)

    @pl.when(i == 0)
    def _():
        fetch(0, 0)

    @pl.when(i + 1 < n_steps)
    def _():
        fetch(i + 1, (i + 1) % 2)

    b = i % 2
    pltpu.make_async_copy(ys_hbm.at[pl.ds(0, 2 * tm)], ybuf.at[b], gsem.at[b]).wait()
    return ybuf.at[b]


def _combined_residual(x_ref, meta_ref, yb):
    tm = x_ref.shape[0]
    meta = meta_ref[...]
    return x_ref[...] + meta[:, 2:3] * yb[0:tm, :] + meta[:, 3:4] * yb[tm:2 * tm, :]


def _combine_inproj_kernel(x_ref, meta_ref, slots_hbm, ys_hbm, g_ref, w_ref, xo_ref, u_ref, s_ref,
                           sl_smem, ybuf, csem, gsem):
    i = pl.program_id(0)
    yb = _gather_expert_rows(i, pl.num_programs(0), x_ref.shape[0], slots_hbm, ys_hbm, sl_smem, ybuf, csem, gsem)
    x = _combined_residual(x_ref, meta_ref, yb)
    xo_ref[...] = x
    _inproj_body(x, g_ref, w_ref, u_ref, s_ref)


def _combine_final_kernel(x_ref, meta_ref, slots_hbm, ys_hbm, g_ref, o_ref, sl_smem, ybuf, csem, gsem):
    i = pl.program_id(0)
    yb = _gather_expert_rows(i, pl.num_programs(0), x_ref.shape[0], slots_hbm, ys_hbm, sl_smem, ybuf, csem, gsem)
    o_ref[...] = _rms(_combined_residual(x_ref, meta_ref, yb), g_ref[...])


def _combine_scratch(tm, dm):
    return [pltpu.SMEM((2, SUBLANES, tm), I32), pltpu.VMEM((2, 2 * tm, dm), F32),
            pltpu.SemaphoreType.DMA, pltpu.SemaphoreType.DMA((2,))]


def _combine_in_projection(x, meta, slots, ys, g, w_bf, conv_width):
    n, dm = x.shape
    tm = slots.shape[-1]
    in_w = w_bf.shape[1]
    sw = in_w - 2 * conv_width
    row = lambda i: (i, 0)
    fixed = lambda i: (0, 0)
    hbm = pl.BlockSpec(memory_space=pl.ANY)
    return pl.pallas_call(
        _combine_inproj_kernel, grid=(n // tm,),
        in_specs=[pl.BlockSpec((tm, dm), row), pl.BlockSpec((tm, LANES), row), hbm, hbm,
                  pl.BlockSpec((1, dm), fixed), pl.BlockSpec((dm, in_w), fixed)],
        out_specs=[pl.BlockSpec((tm, dm), row), pl.BlockSpec((tm, conv_width), row), pl.BlockSpec((tm, sw), row)],
        out_shape=[jax.ShapeDtypeStruct((n, dm), F32), jax.ShapeDtypeStruct((n, conv_width), F32),
                   jax.ShapeDtypeStruct((n, sw), F32)],
        scratch_shapes=_combine_scratch(tm, dm),
        compiler_params=_params(("arbitrary",), disable_bounds_checks=True), name="combine_in_projection",
    )(x, meta, slots, ys, g, w_bf)


def _combine_final_norm(x, meta, slots, ys, g):
    n, dm = x.shape
    tm = slots.shape[-1]
    row = lambda i: (i, 0)
    hbm = pl.BlockSpec(memory_space=pl.ANY)
    return pl.pallas_call(
        _combine_final_kernel, grid=(n // tm,),
        in_specs=[pl.BlockSpec((tm, dm), row), pl.BlockSpec((tm, LANES), row), hbm, hbm,
                  pl.BlockSpec((1, dm), lambda i: (0, 0))],
        out_specs=pl.BlockSpec((tm, dm), row),
        out_shape=jax.ShapeDtypeStruct((n, dm), F32),
        scratch_shapes=_combine_scratch(tm, dm),
        compiler_params=_params(("arbitrary",), disable_bounds_checks=True), name="combine_final_norm",
    )(x, meta, slots, ys, g)


CONV_CHUNK = 32


def _conv_kernel(halo_ref, u_ref, w_ref, b_ref, lng_ref, lnb_ref, gb_ref, o_ref, buf_ref, *, bsz):
    rows = u_ref.shape[0]
    halo = halo_ref.shape[0]
    pieces = CONV_CHUNK // SUBLANES
    i = pl.program_id(0)

    @pl.when(i == 0)
    def _():
        buf_ref[0:halo, :] = jnp.zeros(halo_ref.shape, F32)

    @pl.when(i > 0)
    def _():
        buf_ref[0:halo, :] = halo_ref[...]

    buf_ref[halo:, :] = u_ref[...]

    def chunk(c, carry):
        r0 = pl.multiple_of(c * CONV_CHUNK, CONV_CHUNK)
        acc = [jnp.zeros((SUBLANES, u_ref.shape[1]), F32) for _ in range(pieces)]
        for k in range(CONV_KERNEL):
            off = halo - (CONV_KERNEL - 1 - k) * bsz
            wk = w_ref[k * SUBLANES:(k + 1) * SUBLANES, :]
            for p in range(pieces):
                acc[p] = acc[p] + buf_ref[pl.ds(r0 + off + p * SUBLANES, SUBLANES), :] * wk
        y = jnp.concatenate(acc, axis=0) + b_ref[...]
        mu = jnp.mean(y, axis=-1, keepdims=True)
        yc = y - mu
        yn = yc * lax.rsqrt(jnp.mean(yc * yc, axis=-1, keepdims=True) + EPS)
        yn = yn * lng_ref[...] + lnb_ref[...]
        act = yn * jax.nn.sigmoid(yn)
        o_ref[pl.ds(r0, CONV_CHUNK), :] = _rms(act, gb_ref[...])
        return carry

    lax.fori_loop(0, rows // CONV_CHUNK, chunk, 0)


def _conv_branch(u, w_rep, b, ln_g, ln_b, gb, bsz):
    n, cw = u.shape
    halo = HALO_STEPS * bsz
    rows = min(ROW_TILE, n)
    assert rows % halo == 0 and halo >= (CONV_KERNEL - 1) * bsz
    ratio = rows // halo
    fixed = lambda i: (0, 0)
    return pl.pallas_call(
        functools.partial(_conv_kernel, bsz=bsz),
        grid=(n // rows,),
        in_specs=[pl.BlockSpec((halo, cw), lambda i: (jnp.maximum(i * ratio - 1, 0), 0)),
                  pl.BlockSpec((rows, cw), lambda i: (i, 0)),
                  pl.BlockSpec((CONV_KERNEL * SUBLANES, cw), fixed),
                  pl.BlockSpec((1, cw), fixed), pl.BlockSpec((1, cw), fixed),
                  pl.BlockSpec((1, cw), fixed), pl.BlockSpec((1, cw), fixed)],
        out_specs=pl.BlockSpec((rows, cw), lambda i: (i, 0)),
        out_shape=jax.ShapeDtypeStruct((n, cw), F32),
        scratch_shapes=[pltpu.VMEM((halo + rows, cw), F32)],
        compiler_params=_params(("parallel",)), name="conv_branch",
    )(u, u, w_rep, b, ln_g, ln_b, gb)


SCAN_TILES = 4
SCAN_UNROLL = 2


def _ssm_kernel(s_ref, wd_ref, are_ref, aim_ref, wc_ref, dvec_ref, gw_ref, gbias_ref, gb_ref,
                o_ref, d_ref, hb_ref, h_ref, *, bsz):
    rows, sw = s_ref.shape
    n_state = d_ref.shape[1]
    n_tiles = n_state // MXU_COLS
    tiles_per_lane_block = LANES // (2 * SSM_GROUP)
    halves = bsz // SUBLANES
    steps = rows // bsz

    @pl.when(pl.program_id(0) == 0)
    def _():
        h_ref[...] = jnp.zeros(h_ref.shape, F32)

    s = s_ref[...]
    s_bf = s.astype(BF16)
    for j in range(n_tiles):
        q = j // tiles_per_lane_block
        d_ref[:, j * MXU_COLS:(j + 1) * MXU_COLS] = jnp.dot(
            s_bf[:, q * LANES:(q + 1) * LANES], wd_ref[j], preferred_element_type=F32)

    for jq in range(n_tiles // SCAN_TILES):
        cols = [(jq * SCAN_TILES + jj) * MXU_COLS for jj in range(SCAN_TILES)]
        init = []
        for c in cols:
            for hf in range(halves):
                r = hf * SUBLANES
                init.append(h_ref[r:r + SUBLANES, c:c + LANES])
                init.append(h_ref[r:r + SUBLANES, c + LANES:c + 2 * LANES])

        def step(t, carry, cols=cols):
            row0 = pl.multiple_of(t * bsz, bsz)
            new = []
            k = 0
            for c in cols:
                a_re = jnp.broadcast_to(are_ref[:, c // 2:c // 2 + LANES], (SUBLANES, LANES))
                a_im = jnp.broadcast_to(aim_ref[:, c // 2:c // 2 + LANES], (SUBLANES, LANES))
                res_re, res_im = [], []
                for hf in range(halves):
                    r0 = row0 + hf * SUBLANES
                    hr, hi = carry[k], carry[k + 1]
                    k += 2
                    dr = d_ref[pl.ds(r0, SUBLANES), c:c + LANES]
                    di = d_ref[pl.ds(r0, SUBLANES), c + LANES:c + 2 * LANES]
                    nr = a_re * hr - a_im * hi + dr
                    ni = a_re * hi + a_im * hr + di
                    new += [nr, ni]
                    res_re.append(nr)
                    res_im.append(ni)
                hb_ref[pl.ds(row0, bsz), c:c + LANES] = jnp.concatenate(res_re, axis=0).astype(BF16)
                hb_ref[pl.ds(row0, bsz), c + LANES:c + 2 * LANES] = jnp.concatenate(res_im, axis=0).astype(BF16)
            return tuple(new)

        fin = lax.fori_loop(0, steps, step, tuple(init), unroll=SCAN_UNROLL)
        k = 0
        for c in cols:
            for hf in range(halves):
                r = hf * SUBLANES
                h_ref[r:r + SUBLANES, c:c + LANES] = fin[k]
                h_ref[r:r + SUBLANES, c + LANES:c + 2 * LANES] = fin[k + 1]
                k += 2

    n_out = sw // LANES
    kw = n_state // n_out
    ys = [jnp.dot(hb_ref[:, q * kw:(q + 1) * kw], wc_ref[q], preferred_element_type=F32) for q in range(n_out)]
    y = jnp.concatenate(ys, axis=-1) + dvec_ref[...] * s
    act = jax.nn.gelu(y)
    z = jnp.dot(act.astype(BF16), gw_ref[...], preferred_element_type=F32) + gbias_ref[...]
    o_ref[...] = _rms(act * jax.nn.sigmoid(z), gb_ref[...])


def _ssm_weights(lam_re, lam_im, log_dt, b_re, b_im, c_re, c_im):
    g, n = lam_re.shape
    dt = jnp.exp(log_dt)[:, None]
    mag = jnp.exp(lam_re * dt)
    ab_re = mag * jnp.cos(lam_im * dt)
    ab_im = mag * jnp.sin(lam_im * dt)
    den = lam_re * lam_re + lam_im * lam_im
    nr = ab_re - 1.0
    q_re = (nr * lam_re + ab_im * lam_im) / den
    q_im = (ab_im * lam_re - nr * lam_im) / den
    bb_re = q_re[..., None] * b_re - q_im[..., None] * b_im
    bb_im = q_re[..., None] * b_im + q_im[..., None] * b_re
    sel = jnp.eye(g, dtype=F32).reshape(g, g // 2, 2)
    bbp = jnp.stack([bb_re, bb_im])
    wd = jnp.einsum('gji,pgnc->gcjpin', sel, bbp).reshape(g * SSM_GROUP, 2 * g * n)
    ccp = jnp.stack([c_re, -c_im])
    wc = jnp.einsum('gji,pgcn->jpingc', sel, ccp).reshape(2 * g * n, g * SSM_GROUP)
    n_tiles = 2 * g * n // MXU_COLS
    per_block = LANES // (2 * SSM_GROUP)
    wd_t = jnp.stack([wd[(j // per_block) * LANES:(j // per_block + 1) * LANES,
                         j * MXU_COLS:(j + 1) * MXU_COLS] for j in range(n_tiles)])
    n_out = g * SSM_GROUP // LANES
    kw = 2 * g * n // n_out
    wc_t = jnp.stack([wc[q * kw:(q + 1) * kw, q * LANES:(q + 1) * LANES] for q in range(n_out)])
    a_re = ab_re.reshape(1, g * n)
    a_im = ab_im.reshape(1, g * n)
    return wd_t.astype(BF16), a_re, a_im, wc_t.astype(BF16)


def _ssm_branch(s_in, wd, a_re, a_im, wc, dvec, glu_w_bf, glu_b, gb, bsz):
    n, sw = s_in.shape
    rows = min(ROW_TILE, n)
    n_state = 2 * a_re.shape[1]
    fixed2 = lambda i: (0, 0)
    fixed3 = lambda i: (0, 0, 0)
    vec = pl.BlockSpec((1, sw), fixed2)
    return pl.pallas_call(
        functools.partial(_ssm_kernel, bsz=bsz),
        grid=(n // rows,),
        in_specs=[pl.BlockSpec((rows, sw), lambda i: (i, 0)),
                  pl.BlockSpec(wd.shape, fixed3),
                  pl.BlockSpec(a_re.shape, fixed2), pl.BlockSpec(a_im.shape, fixed2),
                  pl.BlockSpec(wc.shape, fixed3),
                  vec, pl.BlockSpec((sw, sw), fixed2), vec, vec],
        out_specs=pl.BlockSpec((rows, sw), lambda i: (i, 0)),
        out_shape=jax.ShapeDtypeStruct((n, sw), F32),
        scratch_shapes=[pltpu.VMEM((rows, n_state), F32), pltpu.VMEM((rows, n_state), BF16),
                        pltpu.VMEM((bsz, n_state), F32)],
        compiler_params=_params(("arbitrary",)), name="ssm_branch",
    )(s_in, wd, a_re, a_im, wc, dvec, glu_w_bf, glu_b, gb)


def _outproj_router_kernel(x_ref, yc_ref, ys_ref, wo_ref, g_ref, wr_hi_ref, wr_lo_ref, br_ref,
                           xo_ref, h_ref, meta_ref, cnt_ref):
    ycat = jnp.concatenate([yc_ref[...].astype(BF16), ys_ref[...].astype(BF16)], axis=-1)
    x = x_ref[...] + jnp.dot(ycat, wo_ref[...], preferred_element_type=F32)
    xo_ref[...] = x
    h = _rms(x, g_ref[...])
    h_ref[...] = h
    h_hi = h.astype(BF16)
    h_lo = (h - h_hi.astype(F32)).astype(BF16)
    w_hi = wr_hi_ref[...]
    lg = (jnp.dot(h_hi, w_hi, preferred_element_type=F32)
          + jnp.dot(h_lo, w_hi, preferred_element_type=F32)
          + jnp.dot(h_hi, wr_lo_ref[...], preferred_element_type=F32)) + br_ref[...]
    lane = lax.broadcasted_iota(I32, lg.shape, 1).astype(F32)
    ninf = jnp.float32(-jnp.inf)
    big = jnp.float32(LANES)
    is_g = lane < MOE_GROUPS
    gl = jnp.where(is_g, lg, ninf)
    gmax = jnp.max(gl, axis=-1, keepdims=True)
    grp = jnp.min(jnp.where(gl == gmax, lane, big), axis=-1, keepdims=True)
    den = jnp.sum(jnp.where(is_g, jnp.exp(gl - gmax), 0.0), axis=-1, keepdims=True)
    p_grp = 1.0 / den
    lo = MOE_GROUPS + EXPERTS_PER_GROUP * grp
    el = jnp.where((lane >= lo) & (lane < lo + EXPERTS_PER_GROUP), lg, ninf)
    v1 = jnp.max(el, axis=-1, keepdims=True)
    i1 = jnp.min(jnp.where(el == v1, lane, big), axis=-1, keepdims=True)
    el2 = jnp.where(lane == i1, ninf, el)
    v2 = jnp.max(el2, axis=-1, keepdims=True)
    i2 = jnp.min(jnp.where(el2 == v2, lane, big), axis=-1, keepdims=True)
    t = jnp.exp(v2 - v1)
    g1 = p_grp / (1.0 + t)
    g2 = p_grp * t / (1.0 + t)
    e1 = i1 - MOE_GROUPS
    e2 = i2 - MOE_GROUPS
    meta_ref[...] = jnp.where(lane == 0, e1, jnp.where(lane == 1, e2,
                                                       jnp.where(lane == 2, g1, jnp.where(lane == 3, g2, 0.0))))
    hits = jnp.where(lane == e1, 1.0, 0.0) + jnp.where(lane == e2, 1.0, 0.0)
    cnt_ref[...] = jnp.broadcast_to(jnp.sum(hits, axis=0, keepdims=True), cnt_ref.shape)


def _outproj_router(x, yc, ys, wo_bf, g, wr_hi, wr_lo, br):
    n, dm = x.shape
    cw = yc.shape[1]
    sw = ys.shape[1]
    tm = min(ROW_TILE, n)
    row = lambda i: (i, 0)
    fixed = lambda i: (0, 0)
    return pl.pallas_call(
        _outproj_router_kernel, grid=(n // tm,),
        in_specs=[pl.BlockSpec((tm, dm), row), pl.BlockSpec((tm, cw), row), pl.BlockSpec((tm, sw), row),
                  pl.BlockSpec((cw + sw, dm), fixed), pl.BlockSpec((1, dm), fixed),
                  pl.BlockSpec((dm, LANES), fixed), pl.BlockSpec((dm, LANES), fixed),
                  pl.BlockSpec((1, LANES), fixed)],
        out_specs=[pl.BlockSpec((tm, dm), row), pl.BlockSpec((tm, dm), row), pl.BlockSpec((tm, LANES), row),
                   pl.BlockSpec((SUBLANES, LANES), row)],
        out_shape=[jax.ShapeDtypeStruct((n, dm), F32), jax.ShapeDtypeStruct((n, dm), F32),
                   jax.ShapeDtypeStruct((n, LANES), F32),
                   jax.ShapeDtypeStruct((n // tm * SUBLANES, LANES), F32)],
        compiler_params=_params(("parallel",)), name="out_projection_router",
    )(x, yc, ys, wo_bf, g, wr_hi, wr_lo, br)


def _dispatch_kernel(fill_lo_ref, fill_hi_ref, nused_ref, meta_ref, base_ref, tril_ref, h_hbm, xs_hbm, slots_ref,
                     sl_vmem, sl_smem, zbuf, csem, dsem, zsem):
    i = pl.program_id(0)
    n_steps = pl.num_programs(0)
    tm = meta_ref.shape[0]
    meta = meta_ref[...]
    lane = lax.broadcasted_iota(I32, meta.shape, 1).astype(F32)
    oh0 = lane == meta[:, 0:1]
    oh1 = lane == meta[:, 1:2]
    hits = (jnp.where(oh0, 1.0, 0.0) + jnp.where(oh1, 1.0, 0.0)).astype(BF16)
    earlier = jnp.dot(tril_ref[...], hits, preferred_element_type=F32)
    pos = base_ref[0:1, :] + earlier
    s0 = jnp.sum(jnp.where(oh0, pos, 0.0), axis=-1, keepdims=True)
    s1 = jnp.sum(jnp.where(oh1, pos, 0.0), axis=-1, keepdims=True)
    slot_cols = jnp.where(lane == 0, s0, jnp.where(lane == 1, s1, 0.0))
    slot_rows = jnp.transpose(slot_cols)[0:SUBLANES, :].astype(I32)
    slots_ref[...] = slot_rows
    sl_vmem[...] = slot_rows
    b = i % 2
    cp = pltpu.make_async_copy(sl_vmem, sl_smem.at[b], csem)
    cp.start()
    cp.wait()

    def issue(r, carry):
        row = i * tm + r
        s0_ = sl_smem[b, 0, r]
        s1_ = sl_smem[b, 1, r]
        pltpu.make_async_copy(h_hbm.at[pl.ds(row, 1)], xs_hbm.at[pl.ds(s0_, 1)], dsem.at[b]).start()
        pltpu.make_async_copy(h_hbm.at[pl.ds(row, 1)], xs_hbm.at[pl.ds(s1_, 1)], dsem.at[b]).start()
        return carry

    lax.fori_loop(0, tm, issue, 0, unroll=8)

    def wait_tile(bb):
        pltpu.make_async_copy(h_hbm.at[pl.ds(0, 2 * tm)], xs_hbm.at[pl.ds(0, 2 * tm)], dsem.at[bb]).wait()

    @pl.when(i > 0)
    def _():
        wait_tile(1 - b)

    @pl.when(i == n_steps - 1)
    def _():
        wait_tile(b)
        zbuf[...] = jnp.zeros(zbuf.shape, F32)
        n_exp = fill_lo_ref.shape[0]

        def each_pad_row(fn):
            def per_expert(e, carry):
                def per_row(r, c2):
                    fn(r)
                    return c2
                lax.fori_loop(fill_lo_ref[e], fill_hi_ref[e], per_row, 0)
                return carry
            lax.fori_loop(0, n_exp, per_expert, 0)

        each_pad_row(lambda r: pltpu.make_async_copy(
            zbuf.at[pl.ds(0, 1)], xs_hbm.at[pl.ds(r, 1)], zsem).start())
        each_pad_row(lambda r: pltpu.make_async_copy(
            zbuf.at[pl.ds(0, 1)], xs_hbm.at[pl.ds(0, 1)], zsem).wait())

        blk = zbuf.shape[0]

        def each_spare_block(fn):
            def per_block(j, carry):
                fn(pl.multiple_of(j * blk, blk))
                return carry
            lax.fori_loop(nused_ref[0], xs_hbm.shape[0] // blk, per_block, 0)

        each_spare_block(lambda r: pltpu.make_async_copy(zbuf, xs_hbm.at[pl.ds(r, blk)], zsem).start())
        each_spare_block(lambda r: pltpu.make_async_copy(zbuf, xs_hbm.at[pl.ds(0, blk)], zsem).wait())


def _dispatch(h, meta, base, fill_lo, fill_hi, n_used, cap):
    n, dm = h.shape
    tm = min(ROW_TILE, n)
    n_tiles = n // tm
    tril = jnp.tril(jnp.ones((tm, tm), F32), -1).astype(BF16)
    hbm = pl.BlockSpec(memory_space=pl.ANY)
    grid_spec = pltpu.PrefetchScalarGridSpec(
        num_scalar_prefetch=3, grid=(n_tiles,),
        in_specs=[pl.BlockSpec((tm, LANES), lambda i, lo, hi, nu: (i, 0)),
                  pl.BlockSpec((SUBLANES, LANES), lambda i, lo, hi, nu: (i, 0)),
                  pl.BlockSpec((tm, tm), lambda i, lo, hi, nu: (0, 0)),
                  hbm],
        out_specs=[hbm, pl.BlockSpec((None, SUBLANES, tm), lambda i, lo, hi, nu: (i, 0, 0))],
        scratch_shapes=[pltpu.VMEM((SUBLANES, tm), I32), pltpu.SMEM((2, SUBLANES, tm), I32),
                        pltpu.VMEM((MOE_BLOCK, dm), F32),
                        pltpu.SemaphoreType.DMA, pltpu.SemaphoreType.DMA((2,)), pltpu.SemaphoreType.DMA],
    )
    return pl.pallas_call(
        _dispatch_kernel, grid_spec=grid_spec,
        out_shape=[jax.ShapeDtypeStruct((cap, dm), F32), jax.ShapeDtypeStruct((n_tiles, SUBLANES, tm), I32)],
        compiler_params=_params(("arbitrary",), disable_bounds_checks=True), name="dispatch",
    )(fill_lo, fill_hi, n_used, meta, base, tril, h)


def _expert_kernel(be_ref, nused_ref, xs_ref, wgu_ref, wd_ref, ys_ref):
    j = pl.program_id(0)
    de = wd_ref.shape[0]

    @pl.when(j < nused_ref[0])
    def _():
        gu = jnp.dot(xs_ref[...].astype(BF16), wgu_ref[...], preferred_element_type=F32)
        gate = gu[:, :de]
        hid = gate * jax.nn.sigmoid(gate) * gu[:, de:]
        ys_ref[...] = jnp.dot(hid.astype(BF16), wd_ref[...], preferred_element_type=F32)

    @pl.when(j >= nused_ref[0])
    def _():
        ys_ref[...] = jnp.zeros(ys_ref.shape, F32)


def _experts(xs, block_expert, n_used, wgu_bf, wd_bf):
    cap, dm = xs.shape
    blk = MOE_BLOCK
    de2 = wgu_bf.shape[-1]
    grid_spec = pltpu.PrefetchScalarGridSpec(
        num_scalar_prefetch=2, grid=(cap // blk,),
        in_specs=[pl.BlockSpec((blk, dm), lambda j, be, nu: (jnp.minimum(j, nu[0] - 1), 0)),
                  pl.BlockSpec((None, dm, de2), lambda j, be, nu: (be[j], 0, 0)),
                  pl.BlockSpec((None, de2 // 2, dm), lambda j, be, nu: (be[j], 0, 0))],
        out_specs=pl.BlockSpec((blk, dm), lambda j, be, nu: (j, 0)),
    )
    return pl.pallas_call(
        _expert_kernel, grid_spec=grid_spec,
        out_shape=jax.ShapeDtypeStruct((cap, dm), F32),
        compiler_params=_params(("arbitrary",)), name="expert_blocks",
    )(block_expert, n_used, xs, wgu_bf, wd_bf)


def _moe(h, meta, cnt, wgu_bf, wd_bf):
    n = h.shape[0]
    n_exp = wgu_bf.shape[0]
    blk = MOE_BLOCK
    n_blocks = -(-2 * n // blk) + n_exp
    tile_cnt = cnt.reshape(-1, SUBLANES, LANES)[:, 0, :].astype(I32)
    counts = jnp.sum(tile_cnt, axis=0)
    padded = (counts + blk - 1) // blk * blk
    pad_ends = jnp.cumsum(padded)
    pad_starts = pad_ends - padded
    before = jnp.cumsum(tile_cnt, axis=0) - tile_cnt
    base = jnp.repeat((pad_starts[None, :] + before).astype(F32), SUBLANES, axis=0)
    fill_lo = (pad_starts + counts)[:n_exp]
    fill_hi = pad_ends[:n_exp]
    n_used = pad_ends[n_exp - 1] // blk
    block_start = jnp.minimum(jnp.arange(n_blocks, dtype=I32), n_used - 1) * blk
    block_expert = jnp.minimum(jnp.searchsorted(pad_ends[:n_exp], block_start, side='right'),
                               n_exp - 1).astype(I32)
    n_used = n_used.astype(I32).reshape(1)
    xs, slots = _dispatch(h, meta, base, fill_lo, fill_hi, n_used, n_blocks * blk)
    ys = _experts(xs, block_expert, n_used, wgu_bf, wd_bf)
    return ys, slots


def kernel(x, g_mix, w_in, conv_w, conv_b, conv_ln_g, conv_ln_b, ssm_lam_re, ssm_lam_im, ssm_log_dt,
           ssm_b_re, ssm_b_im, ssm_c_re, ssm_c_im, ssm_d, ssm_glu_w, ssm_glu_b, g_branch, w_out, g_ffn,
           w_router_group, b_router_group, w_router_expert, b_router_expert, w_gate, w_up, w_down, g_final):
    bsz, seq, dm = x.shape
    depth = w_in.shape[0]
    cw = conv_w.shape[-1]
    assert bsz % SUBLANES == 0
    row2 = lambda v: v.reshape(1, -1)

    xt = _to_time_major(x)
    moe = None
    for l in range(depth):
        w_in_bf = w_in[l].astype(BF16)
        if moe is None:
            u, s_in = _in_projection(xt, row2(g_mix[l]), w_in_bf, cw)
        else:
            xt, u, s_in = _combine_in_projection(xt, *moe, row2(g_mix[l]), w_in_bf, cw)
        gb = g_branch[l]
        yc = _conv_branch(u, jnp.repeat(conv_w[l], SUBLANES, axis=0), row2(conv_b[l]), row2(conv_ln_g[l]),
                          row2(conv_ln_b[l]), row2(gb[:cw]), bsz)
        wd, a_re, a_im, wc = _ssm_weights(ssm_lam_re[l], ssm_lam_im[l], ssm_log_dt[l],
                                          ssm_b_re[l], ssm_b_im[l], ssm_c_re[l], ssm_c_im[l])
        ys = _ssm_branch(s_in, wd, a_re, a_im, wc, row2(ssm_d[l]), ssm_glu_w[l].astype(BF16),
                         row2(ssm_glu_b[l]), row2(gb[cw:]), bsz)
        w_r = jnp.concatenate([w_router_group[l], w_router_expert[l]], axis=1)
        w_r = jnp.pad(w_r, ((0, 0), (0, LANES - w_r.shape[1])))
        wr_hi = w_r.astype(BF16)
        wr_lo = (w_r - wr_hi.astype(F32)).astype(BF16)
        b_r = jnp.concatenate([b_router_group[l], b_router_expert[l]])
        b_r = jnp.pad(b_r, (0, LANES - b_r.shape[0])).reshape(1, LANES)
        xt, h, meta, cnt = _outproj_router(xt, yc, ys, w_out[l].astype(BF16), row2(g_ffn[l]), wr_hi, wr_lo, b_r)
        wgu = jnp.concatenate([w_gate[l], w_up[l]], axis=-1).astype(BF16)
        y_sorted, slots = _moe(h, meta, cnt, wgu, w_down[l].astype(BF16))
        moe = (meta, slots, y_sorted)
    out_tm = _combine_final_norm(xt, *moe, row2(g_final))
    return _from_time_major(out_tm, bsz)
```

```python
import functools

import jax
import jax.numpy as jnp
from jax import lax
from jax.experimental import pallas as pl
from jax.experimental.pallas import tpu as pltpu

F32 = jnp.float32
BF16 = jnp.bfloat16
I32 = jnp.int32

EPS = 1e-6
CONV_KERNEL = 31
SSM_GROUP = 16
SSM_STATE = 64
MOE_GROUPS = 4
EXPERTS_PER_GROUP = 8
N_EXPERTS = MOE_GROUPS * EXPERTS_PER_GROUP
LANES = 128
SUBLANES = 8
MXU_COLS = 256
VMEM_LIMIT = 56 * 1024 * 1024

ROW_TILE = 512
HALO_STEPS = 32
MOE_BLOCK = 256
LAYOUT_STEPS = 256


def _params(sem, **kw):
    return pltpu.CompilerParams(dimension_semantics=sem, vmem_limit_bytes=VMEM_LIMIT, **kw)


def _rms(x, g):
    return x * lax.rsqrt(jnp.mean(x * x, axis=-1, keepdims=True) + EPS) * g


def _to_tm_kernel(x_ref, o_ref):
    bsz, tt, _ = x_ref.shape
    for b in range(bsz):
        o_ref[pl.ds(b, tt, stride=bsz), :] = x_ref[b]


def _from_tm_kernel(x_ref, o_ref):
    bsz, tt, _ = o_ref.shape
    for b in range(bsz):
        o_ref[b] = x_ref[pl.ds(b, tt, stride=bsz), :]


def _to_time_major(x):
    bsz, seq, dm = x.shape
    tt = min(seq, LAYOUT_STEPS)
    return pl.pallas_call(
        _to_tm_kernel, grid=(seq // tt, dm // LANES),
        in_specs=[pl.BlockSpec((bsz, tt, LANES), lambda i, c: (0, i, c))],
        out_specs=pl.BlockSpec((tt * bsz, LANES), lambda i, c: (i, c)),
        out_shape=jax.ShapeDtypeStruct((seq * bsz, dm), x.dtype),
        compiler_params=_params(("parallel", "parallel")), name="to_time_major",
    )(x)


def _from_time_major(xt, bsz):
    n, dm = xt.shape
    seq = n // bsz
    tt = min(seq, LAYOUT_STEPS)
    return pl.pallas_call(
        _from_tm_kernel, grid=(seq // tt, dm // LANES),
        in_specs=[pl.BlockSpec((tt * bsz, LANES), lambda i, c: (i, c))],
        out_specs=pl.BlockSpec((bsz, tt, LANES), lambda i, c: (0, i, c)),
        out_shape=jax.ShapeDtypeStruct((bsz, seq, dm), xt.dtype),
        compiler_params=_params(("parallel", "parallel")), name="from_time_major",
    )(xt)


def _inproj_body(x, g_ref, w_ref, u_ref, s_ref):
    cw = u_ref.shape[-1]
    h = _rms(x, g_ref[...]).astype(BF16)
    z = jnp.dot(h, w_ref[...], preferred_element_type=F32)
    u_ref[...] = z[:, :cw] * jax.nn.sigmoid(z[:, cw:2 * cw])
    s_ref[...] = z[:, 2 * cw:]


def _inproj_kernel(x_ref, g_ref, w_ref, u_ref, s_ref):
    _inproj_body(x_ref[...], g_ref, w_ref, u_ref, s_ref)


def _in_projection(x, g, w_bf, conv_width):
    n, dm = x.shape
    tm = min(ROW_TILE, n)
    in_w = w_bf.shape[1]
    sw = in_w - 2 * conv_width
    row = lambda i: (i, 0)
    fixed = lambda i: (0, 0)
    return pl.pallas_call(
        _inproj_kernel, grid=(n // tm,),
        in_specs=[pl.BlockSpec((tm, dm), row), pl.BlockSpec((1, dm), fixed), pl.BlockSpec((dm, in_w), fixed)],
        out_specs=[pl.BlockSpec((tm, conv_width), row), pl.BlockSpec((tm, sw), row)],
        out_shape=[jax.ShapeDtypeStruct((n, conv_width), F32), jax.ShapeDtypeStruct((n, sw), F32)],
        compiler_params=_params(("parallel",)), name="in_projection",
    )(x, g, w_bf)


def _token_rows(ref, first, n_rows, lead=()):
    return jnp.concatenate(
        [ref[lead + (pl.ds(first * SUBLANES + s, n_rows, stride=SUBLANES), slice(None))] for s in range(SUBLANES)],
        axis=-1)


def _store_token_rows(ref, val):
    n_rows = val.shape[0]
    for s in range(SUBLANES):
        ref[pl.ds(s, n_rows, stride=SUBLANES), :] = val[:, s * LANES:(s + 1) * LANES]


def _gather_expert_rows(i, n_steps, tm, slots_hbm, ys_hbm, sl_smem, ybuf, csem, gsem):
    def fetch(tile, b):
        cp = pltpu.make_async_copy(slots_hbm.at[tile], sl_smem.at[b], csem)
        cp.start()
        cp.wait()

        def issue(r, carry):
            for k in range(2):
                src = pl.multiple_of(sl_smem[b, k, r] * SUBLANES, SUBLANES)
                dst = pl.multiple_of((k * tm + r) * SUBLANES, SUBLANES)
                pltpu.make_async_copy(ys_hbm.at[pl.ds(src, SUBLANES)], ybuf.at[b].at[pl.ds(dst, SUBLANES)],
                                      gsem.at[b]).start()
            return carry

        lax.fori_loop(0, tm, issue, 0, unroll=8)

    @pl.when(i == 0)
    def _():
        fetch(0, 0)

    @pl.when(i + 1 < n_steps)
    def _():
        fetch(i + 1, (i + 1) % 2)

    b = i % 2
    pltpu.make_async_copy(ys_hbm.at[pl.ds(0, 2 * tm * SUBLANES)], ybuf.at[b], gsem.at[b]).wait()
    return b


def _combined_residual(x_ref, meta_ref, ybuf, b):
    tm = x_ref.shape[0]
    meta = meta_ref[...]
    y0 = _token_rows(ybuf, 0, tm, lead=(b,))
    y1 = _token_rows(ybuf, tm, tm, lead=(b,))
    return x_ref[...] + meta[:, 2:3] * y0 + meta[:, 3:4] * y1


def _combine_inproj_kernel(x_ref, meta_ref, slots_hbm, ys_hbm, g_ref, w_ref, xo_ref, u_ref, s_ref,
                           sl_smem, ybuf, csem, gsem):
    i = pl.program_id(0)
    b = _gather_expert_rows(i, pl.num_programs(0), x_ref.shape[0], slots_hbm, ys_hbm, sl_smem, ybuf, csem, gsem)
    x = _combined_residual(x_ref, meta_ref, ybuf, b)
    xo_ref[...] = x
    _inproj_body(x, g_ref, w_ref, u_ref, s_ref)


def _combine_final_kernel(x_ref, meta_ref, slots_hbm, ys_hbm, g_ref, o_ref, sl_smem, ybuf, csem, gsem):
    i = pl.program_id(0)
    b = _gather_expert_rows(i, pl.num_programs(0), x_ref.shape[0], slots_hbm, ys_hbm, sl_smem, ybuf, csem, gsem)
    o_ref[...] = _rms(_combined_residual(x_ref, meta_ref, ybuf, b), g_ref[...])


def _combine_scratch(tm, dm):
    tile_rows = dm // LANES
    assert tile_rows == SUBLANES
    return [pltpu.SMEM((2, SUBLANES, tm), I32), pltpu.VMEM((2, 2 * tm * tile_rows, LANES), F32),
            pltpu.SemaphoreType.DMA, pltpu.SemaphoreType.DMA((2,))]


def _combine_in_projection(x, meta, slots, ys, g, w_bf, conv_width):
    n, dm = x.shape
    tm = slots.shape[-1]
    in_w = w_bf.shape[1]
    sw = in_w - 2 * conv_width
    row = lambda i: (i, 0)
    fixed = lambda i: (0, 0)
    hbm = pl.BlockSpec(memory_space=pl.ANY)
    return pl.pallas_call(
        _combine_inproj_kernel, grid=(n // tm,),
        in_specs=[pl.BlockSpec((tm, dm), row), pl.BlockSpec((tm, LANES), row), hbm, hbm,
                  pl.BlockSpec((1, dm), fixed), pl.BlockSpec((dm, in_w), fixed)],
        out_specs=[pl.BlockSpec((tm, dm), row), pl.BlockSpec((tm, conv_width), row), pl.BlockSpec((tm, sw), row)],
        out_shape=[jax.ShapeDtypeStruct((n, dm), F32), jax.ShapeDtypeStruct((n, conv_width), F32),
                   jax.ShapeDtypeStruct((n, sw), F32)],
        scratch_shapes=_combine_scratch(tm, dm),
        compiler_params=_params(("arbitrary",), disable_bounds_checks=True), name="combine_in_projection",
    )(x, meta, slots, ys, g, w_bf)


def _combine_final_norm(x, meta, slots, ys, g):
    n, dm = x.shape
    tm = slots.shape[-1]
    row = lambda i: (i, 0)
    hbm = pl.BlockSpec(memory_space=pl.ANY)
    return pl.pallas_call(
        _combine_final_kernel, grid=(n // tm,),
        in_specs=[pl.BlockSpec((tm, dm), row), pl.BlockSpec((tm, LANES), row), hbm, hbm,
                  pl.BlockSpec((1, dm), lambda i: (0, 0))],
        out_specs=pl.BlockSpec((tm, dm), row),
        out_shape=jax.ShapeDtypeStruct((n, dm), F32),
        scratch_shapes=_combine_scratch(tm, dm),
        compiler_params=_params(("arbitrary",), disable_bounds_checks=True), name="combine_final_norm",
    )(x, meta, slots, ys, g)


CONV_CHUNK = 32
CONV_UNROLL = 2


def _conv_kernel(halo_ref, u_ref, w_ref, b_ref, lng_ref, lnb_ref, gb_ref, o_ref, buf_ref, *, bsz):
    rows = u_ref.shape[0]
    halo = halo_ref.shape[0]
    pieces = CONV_CHUNK // SUBLANES
    i = pl.program_id(0)

    @pl.when(i == 0)
    def _():
        buf_ref[0:halo, :] = jnp.zeros(halo_ref.shape, F32)

    @pl.when(i > 0)
    def _():
        buf_ref[0:halo, :] = halo_ref[...]

    buf_ref[halo:, :] = u_ref[...]

    def chunk(c, carry):
        r0 = pl.multiple_of(c * CONV_CHUNK, CONV_CHUNK)
        first = r0 + halo - (CONV_KERNEL - 1) * bsz

        def tap(k, acc):
            wk = w_ref[pl.ds(pl.multiple_of(k * SUBLANES, SUBLANES), SUBLANES), :]
            xk = buf_ref[pl.ds(pl.multiple_of(first + k * bsz, SUBLANES), CONV_CHUNK), :]
            return acc + xk * jnp.concatenate([wk] * pieces, axis=0)

        acc = lax.fori_loop(0, CONV_KERNEL, tap, jnp.zeros((CONV_CHUNK, u_ref.shape[1]), F32), unroll=CONV_UNROLL)
        o_ref[pl.ds(r0, CONV_CHUNK), :] = acc
        return carry

    lax.fori_loop(0, rows // CONV_CHUNK, chunk, 0)

    y = o_ref[...] + b_ref[...]
    mu = jnp.mean(y, axis=-1, keepdims=True)
    yc = y - mu
    yn = yc * lax.rsqrt(jnp.mean(yc * yc, axis=-1, keepdims=True) + EPS)
    yn = yn * lng_ref[...] + lnb_ref[...]
    act = yn * jax.nn.sigmoid(yn)
    o_ref[...] = _rms(act, gb_ref[...])


def _conv_branch(u, w_rep, b, ln_g, ln_b, gb, bsz):
    n, cw = u.shape
    halo = HALO_STEPS * bsz
    rows = min(ROW_TILE, n)
    assert rows % halo == 0 and halo >= (CONV_KERNEL - 1) * bsz
    ratio = rows // halo
    fixed = lambda i: (0, 0)
    return pl.pallas_call(
        functools.partial(_conv_kernel, bsz=bsz),
        grid=(n // rows,),
        in_specs=[pl.BlockSpec((halo, cw), lambda i: (jnp.maximum(i * ratio - 1, 0), 0)),
                  pl.BlockSpec((rows, cw), lambda i: (i, 0)),
                  pl.BlockSpec((CONV_KERNEL * SUBLANES, cw), fixed),
                  pl.BlockSpec((1, cw), fixed), pl.BlockSpec((1, cw), fixed),
                  pl.BlockSpec((1, cw), fixed), pl.BlockSpec((1, cw), fixed)],
        out_specs=pl.BlockSpec((rows, cw), lambda i: (i, 0)),
        out_shape=jax.ShapeDtypeStruct((n, cw), F32),
        scratch_shapes=[pltpu.VMEM((halo + rows, cw), F32)],
        compiler_params=_params(("parallel",)), name="conv_branch",
    )(u, u, w_rep, b, ln_g, ln_b, gb)


SCAN_TILES = 4
SCAN_UNROLL = 2


def _ssm_kernel(s_ref, wd_ref, are_ref, aim_ref, wc_ref, dvec_ref, gw_ref, gbias_ref, gb_ref,
                o_ref, d_ref, hb_ref, h_ref, *, bsz):
    rows, sw = s_ref.shape
    n_state = d_ref.shape[1]
    n_tiles = n_state // MXU_COLS
    tiles_per_lane_block = LANES // (2 * SSM_GROUP)
    halves = bsz // SUBLANES
    steps = rows // bsz

    @pl.when(pl.program_id(0) == 0)
    def _():
        h_ref[...] = jnp.zeros(h_ref.shape, F32)

    s = s_ref[...]
    s_bf = s.astype(BF16)
    for j in range(n_tiles):
        q = j // tiles_per_lane_block
        d_ref[:, j * MXU_COLS:(j + 1) * MXU_COLS] = jnp.dot(
            s_bf[:, q * LANES:(q + 1) * LANES], wd_ref[j], preferred_element_type=F32)

    for jq in range(n_tiles // SCAN_TILES):
        cols = [(jq * SCAN_TILES + jj) * MXU_COLS for jj in range(SCAN_TILES)]
        init = []
        for c in cols:
            for hf in range(halves):
                r = hf * SUBLANES
                init.append(h_ref[r:r + SUBLANES, c:c + LANES])
                init.append(h_ref[r:r + SUBLANES, c + LANES:c + 2 * LANES])

        def step(t, carry, cols=cols):
            row0 = pl.multiple_of(t * bsz, bsz)
            new = []
            k = 0
            for c in cols:
                a_re = jnp.broadcast_to(are_ref[:, c // 2:c // 2 + LANES], (SUBLANES, LANES))
                a_im = jnp.broadcast_to(aim_ref[:, c // 2:c // 2 + LANES], (SUBLANES, LANES))
                res_re, res_im = [], []
                for hf in range(halves):
                    r0 = row0 + hf * SUBLANES
                    hr, hi = carry[k], carry[k + 1]
                    k += 2
                    dr = d_ref[pl.ds(r0, SUBLANES), c:c + LANES]
                    di = d_ref[pl.ds(r0, SUBLANES), c + LANES:c + 2 * LANES]
                    nr = a_re * hr - a_im * hi + dr
                    ni = a_re * hi + a_im * hr + di
                    new += [nr, ni]
                    res_re.append(nr)
                    res_im.append(ni)
                hb_ref[pl.ds(row0, bsz), c:c + LANES] = jnp.concatenate(res_re, axis=0).astype(BF16)
                hb_ref[pl.ds(row0, bsz), c + LANES:c + 2 * LANES] = jnp.concatenate(res_im, axis=0).astype(BF16)
            return tuple(new)

        fin = lax.fori_loop(0, steps, step, tuple(init), unroll=SCAN_UNROLL)
        k = 0
        for c in cols:
            for hf in range(halves):
                r = hf * SUBLANES
                h_ref[r:r + SUBLANES, c:c + LANES] = fin[k]
                h_ref[r:r + SUBLANES, c + LANES:c + 2 * LANES] = fin[k + 1]
                k += 2

    n_out = sw // LANES
    kw = n_state // n_out
    ys = [jnp.dot(hb_ref[:, q * kw:(q + 1) * kw], wc_ref[q], preferred_element_type=F32) for q in range(n_out)]
    y = jnp.concatenate(ys, axis=-1) + dvec_ref[...] * s
    act = jax.nn.gelu(y)
    z = jnp.dot(act.astype(BF16), gw_ref[...], preferred_element_type=F32) + gbias_ref[...]
    o_ref[...] = _rms(act * jax.nn.sigmoid(z), gb_ref[...])


def _ssm_weights(lam_re, lam_im, log_dt, b_re, b_im, c_re, c_im):
    g, n = lam_re.shape
    dt = jnp.exp(log_dt)[:, None]
    mag = jnp.exp(lam_re * dt)
    ab_re = mag * jnp.cos(lam_im * dt)
    ab_im = mag * jnp.sin(lam_im * dt)
    den = lam_re * lam_re + lam_im * lam_im
    nr = ab_re - 1.0
    q_re = (nr * lam_re + ab_im * lam_im) / den
    q_im = (ab_im * lam_re - nr * lam_im) / den
    bb_re = q_re[..., None] * b_re - q_im[..., None] * b_im
    bb_im = q_re[..., None] * b_im + q_im[..., None] * b_re
    sel = jnp.eye(g, dtype=F32).reshape(g, g // 2, 2)
    bbp = jnp.stack([bb_re, bb_im])
    wd = jnp.einsum('gji,pgnc->gcjpin', sel, bbp).reshape(g * SSM_GROUP, 2 * g * n)
    ccp = jnp.stack([c_re, -c_im])
    wc = jnp.einsum('gji,pgcn->jpingc', sel, ccp).reshape(2 * g * n, g * SSM_GROUP)
    n_tiles = 2 * g * n // MXU_COLS
    per_block = LANES // (2 * SSM_GROUP)
    wd_t = jnp.stack([wd[(j // per_block) * LANES:(j // per_block + 1) * LANES,
                         j * MXU_COLS:(j + 1) * MXU_COLS] for j in range(n_tiles)])
    n_out = g * SSM_GROUP // LANES
    kw = 2 * g * n // n_out
    wc_t = jnp.stack([wc[q * kw:(q + 1) * kw, q * LANES:(q + 1) * LANES] for q in range(n_out)])
    a_re = ab_re.reshape(1, g * n)
    a_im = ab_im.reshape(1, g * n)
    return wd_t.astype(BF16), a_re, a_im, wc_t.astype(BF16)


def _ssm_branch(s_in, wd, a_re, a_im, wc, dvec, glu_w_bf, glu_b, gb, bsz):
    n, sw = s_in.shape
    rows = min(ROW_TILE, n)
    n_state = 2 * a_re.shape[1]
    fixed2 = lambda i: (0, 0)
    fixed3 = lambda i: (0, 0, 0)
    vec = pl.BlockSpec((1, sw), fixed2)
    return pl.pallas_call(
        functools.partial(_ssm_kernel, bsz=bsz),
        grid=(n // rows,),
        in_specs=[pl.BlockSpec((rows, sw), lambda i: (i, 0)),
                  pl.BlockSpec(wd.shape, fixed3),
                  pl.BlockSpec(a_re.shape, fixed2), pl.BlockSpec(a_im.shape, fixed2),
                  pl.BlockSpec(wc.shape, fixed3),
                  vec, pl.BlockSpec((sw, sw), fixed2), vec, vec],
        out_specs=pl.BlockSpec((rows, sw), lambda i: (i, 0)),
        out_shape=jax.ShapeDtypeStruct((n, sw), F32),
        scratch_shapes=[pltpu.VMEM((rows, n_state), F32), pltpu.VMEM((rows, n_state), BF16),
                        pltpu.VMEM((bsz, n_state), F32)],
        compiler_params=_params(("arbitrary",)), name="ssm_branch",
    )(s_in, wd, a_re, a_im, wc, dvec, glu_w_bf, glu_b, gb)


def _outproj_router_kernel(x_ref, yc_ref, ys_ref, wo_ref, g_ref, wr_hi_ref, wr_lo_ref, br_ref,
                           xo_ref, h_ref, meta_ref, cnt_ref):
    ycat = jnp.concatenate([yc_ref[...].astype(BF16), ys_ref[...].astype(BF16)], axis=-1)
    x = x_ref[...] + jnp.dot(ycat, wo_ref[...], preferred_element_type=F32)
    xo_ref[...] = x
    h = _rms(x, g_ref[...])
    _store_token_rows(h_ref, h)
    h_hi = h.astype(BF16)
    h_lo = (h - h_hi.astype(F32)).astype(BF16)
    w_hi = wr_hi_ref[...]
    lg = (jnp.dot(h_hi, w_hi, preferred_element_type=F32)
          + jnp.dot(h_lo, w_hi, preferred_element_type=F32)
          + jnp.dot(h_hi, wr_lo_ref[...], preferred_element_type=F32)) + br_ref[...]
    lane = lax.broadcasted_iota(I32, lg.shape, 1).astype(F32)
    ninf = jnp.float32(-jnp.inf)
    big = jnp.float32(LANES)
    is_g = lane < MOE_GROUPS
    gl = jnp.where(is_g, lg, ninf)
    gmax = jnp.max(gl, axis=-1, keepdims=True)
    grp = jnp.min(jnp.where(gl == gmax, lane, big), axis=-1, keepdims=True)
    den = jnp.sum(jnp.where(is_g, jnp.exp(gl - gmax), 0.0), axis=-1, keepdims=True)
    p_grp = 1.0 / den
    lo = MOE_GROUPS + EXPERTS_PER_GROUP * grp
    el = jnp.where((lane >= lo) & (lane < lo + EXPERTS_PER_GROUP), lg, ninf)
    v1 = jnp.max(el, axis=-1, keepdims=True)
    i1 = jnp.min(jnp.where(el == v1, lane, big), axis=-1, keepdims=True)
    el2 = jnp.where(lane == i1, ninf, el)
    v2 = jnp.max(el2, axis=-1, keepdims=True)
    i2 = jnp.min(jnp.where(el2 == v2, lane, big), axis=-1, keepdims=True)
    t = jnp.exp(v2 - v1)
    g1 = p_grp / (1.0 + t)
    g2 = p_grp * t / (1.0 + t)
    e1 = i1 - MOE_GROUPS
    e2 = i2 - MOE_GROUPS
    meta_ref[...] = jnp.where(lane == 0, e1, jnp.where(lane == 1, e2,
                                                       jnp.where(lane == 2, g1, jnp.where(lane == 3, g2, 0.0))))
    hits = jnp.where(lane == e1, 1.0, 0.0) + jnp.where(lane == e2, 1.0, 0.0)
    cnt_ref[...] = jnp.broadcast_to(jnp.sum(hits, axis=0, keepdims=True), cnt_ref.shape)


def _outproj_router(x, yc, ys, wo_bf, g, wr_hi, wr_lo, br):
    n, dm = x.shape
    cw = yc.shape[1]
    sw = ys.shape[1]
    tm = min(ROW_TILE, n)
    row = lambda i: (i, 0)
    fixed = lambda i: (0, 0)
    return pl.pallas_call(
        _outproj_router_kernel, grid=(n // tm,),
        in_specs=[pl.BlockSpec((tm, dm), row), pl.BlockSpec((tm, cw), row), pl.BlockSpec((tm, sw), row),
                  pl.BlockSpec((cw + sw, dm), fixed), pl.BlockSpec((1, dm), fixed),
                  pl.BlockSpec((dm, LANES), fixed), pl.BlockSpec((dm, LANES), fixed),
                  pl.BlockSpec((1, LANES), fixed)],
        out_specs=[pl.BlockSpec((tm, dm), row), pl.BlockSpec((tm * dm // LANES, LANES), row),
                   pl.BlockSpec((tm, LANES), row), pl.BlockSpec((SUBLANES, LANES), row)],
        out_shape=[jax.ShapeDtypeStruct((n, dm), F32), jax.ShapeDtypeStruct((n * dm // LANES, LANES), F32),
                   jax.ShapeDtypeStruct((n, LANES), F32),
                   jax.ShapeDtypeStruct((n // tm * SUBLANES, LANES), F32)],
        compiler_params=_params(("parallel",)), name="out_projection_router",
    )(x, yc, ys, wo_bf, g, wr_hi, wr_lo, br)


def _dispatch_kernel(fill_lo_ref, fill_hi_ref, nused_ref, meta_ref, base_ref, tril_ref, h_ref, xs_hbm, slots_ref,
                     sl_vmem, sl_smem, zbuf, csem, dsem, zsem):
    i = pl.program_id(0)
    n_steps = pl.num_programs(0)
    tm = meta_ref.shape[0]
    tile = SUBLANES
    meta = meta_ref[...]
    lane = lax.broadcasted_iota(I32, meta.shape, 1).astype(F32)
    oh0 = lane == meta[:, 0:1]
    oh1 = lane == meta[:, 1:2]
    hits = (jnp.where(oh0, 1.0, 0.0) + jnp.where(oh1, 1.0, 0.0)).astype(BF16)
    earlier = jnp.dot(tril_ref[...], hits, preferred_element_type=F32)
    pos = base_ref[0:1, :] + earlier
    s0 = jnp.sum(jnp.where(oh0, pos, 0.0), axis=-1, keepdims=True)
    s1 = jnp.sum(jnp.where(oh1, pos, 0.0), axis=-1, keepdims=True)
    slot_cols = jnp.where(lane == 0, s0, jnp.where(lane == 1, s1, 0.0))
    slot_rows = jnp.transpose(slot_cols)[0:SUBLANES, :].astype(I32)
    slots_ref[...] = slot_rows
    sl_vmem[...] = slot_rows
    cp = pltpu.make_async_copy(sl_vmem, sl_smem, csem)
    cp.start()
    cp.wait()

    def issue(r, carry):
        src = pl.multiple_of(r * tile, tile)
        for k in range(2):
            dst = pl.multiple_of(sl_smem[k, r] * tile, tile)
            pltpu.make_async_copy(h_ref.at[pl.ds(src, tile)], xs_hbm.at[pl.ds(dst, tile)], dsem).start()
        return carry

    lax.fori_loop(0, tm, issue, 0, unroll=8)
    pltpu.make_async_copy(h_ref, xs_hbm.at[pl.ds(0, tm * tile)], dsem).wait()
    pltpu.make_async_copy(h_ref, xs_hbm.at[pl.ds(0, tm * tile)], dsem).wait()

    @pl.when(i == n_steps - 1)
    def _():
        zbuf[...] = jnp.zeros(zbuf.shape, F32)
        n_exp = fill_lo_ref.shape[0]

        def each_pad_row(fn):
            def per_expert(e, carry):
                def per_row(r, c2):
                    fn(pl.multiple_of(r * tile, tile))
                    return c2
                lax.fori_loop(fill_lo_ref[e], fill_hi_ref[e], per_row, 0)
                return carry
            lax.fori_loop(0, n_exp, per_expert, 0)

        each_pad_row(lambda r: pltpu.make_async_copy(
            zbuf.at[pl.ds(0, tile)], xs_hbm.at[pl.ds(r, tile)], zsem).start())
        each_pad_row(lambda r: pltpu.make_async_copy(
            zbuf.at[pl.ds(0, tile)], xs_hbm.at[pl.ds(0, tile)], zsem).wait())

        blk_rows = zbuf.shape[0]

        def each_spare_block(fn):
            def per_block(j, carry):
                fn(pl.multiple_of(j * blk_rows, blk_rows))
                return carry
            lax.fori_loop(nused_ref[0], xs_hbm.shape[0] // blk_rows, per_block, 0)

        each_spare_block(lambda r: pltpu.make_async_copy(zbuf, xs_hbm.at[pl.ds(r, blk_rows)], zsem).start())
        each_spare_block(lambda r: pltpu.make_async_copy(zbuf, xs_hbm.at[pl.ds(0, blk_rows)], zsem).wait())


def _dispatch(h_tiles, meta, base, fill_lo, fill_hi, n_used, cap):
    n = meta.shape[0]
    tile = h_tiles.shape[0] // n
    tm = min(ROW_TILE, n)
    n_tiles = n // tm
    tril = jnp.tril(jnp.ones((tm, tm), F32), -1).astype(BF16)
    hbm = pl.BlockSpec(memory_space=pl.ANY)
    grid_spec = pltpu.PrefetchScalarGridSpec(
        num_scalar_prefetch=3, grid=(n_tiles,),
        in_specs=[pl.BlockSpec((tm, LANES), lambda i, lo, hi, nu: (i, 0)),
                  pl.BlockSpec((SUBLANES, LANES), lambda i, lo, hi, nu: (i, 0)),
                  pl.BlockSpec((tm, tm), lambda i, lo, hi, nu: (0, 0)),
                  pl.BlockSpec((tm * tile, LANES), lambda i, lo, hi, nu: (i, 0))],
        out_specs=[hbm, pl.BlockSpec((None, SUBLANES, tm), lambda i, lo, hi, nu: (i, 0, 0))],
        scratch_shapes=[pltpu.VMEM((SUBLANES, tm), I32), pltpu.SMEM((SUBLANES, tm), I32),
                        pltpu.VMEM((MOE_BLOCK * tile, LANES), F32),
                        pltpu.SemaphoreType.DMA, pltpu.SemaphoreType.DMA, pltpu.SemaphoreType.DMA],
    )
    return pl.pallas_call(
        _dispatch_kernel, grid_spec=grid_spec,
        out_shape=[jax.ShapeDtypeStruct((cap * tile, LANES), F32),
                   jax.ShapeDtypeStruct((n_tiles, SUBLANES, tm), I32)],
        compiler_params=_params(("arbitrary",), disable_bounds_checks=True), name="dispatch",
    )(fill_lo, fill_hi, n_used, meta, base, tril, h_tiles)


def _expert_kernel(be_ref, nused_ref, xs_ref, wg_ref, wu_ref, wd_ref, ys_ref, wg_bf, wu_bf, wd_bf):
    j = pl.program_id(0)
    blk = xs_ref.shape[0] // SUBLANES

    @pl.when((j == 0) | (be_ref[j] != be_ref[jnp.maximum(j - 1, 0)]))
    def _():
        wg_bf[...] = wg_ref[...].astype(BF16)
        wu_bf[...] = wu_ref[...].astype(BF16)
        wd_bf[...] = wd_ref[...].astype(BF16)

    @pl.when(j < nused_ref[0])
    def _():
        x = _token_rows(xs_ref, 0, blk).astype(BF16)
        gate = jnp.dot(x, wg_bf[...], preferred_element_type=F32)
        up = jnp.dot(x, wu_bf[...], preferred_element_type=F32)
        hid = gate * jax.nn.sigmoid(gate) * up
        _store_token_rows(ys_ref, jnp.dot(hid.astype(BF16), wd_bf[...], preferred_element_type=F32))

    @pl.when(j >= nused_ref[0])
    def _():
        ys_ref[...] = jnp.zeros(ys_ref.shape, F32)


def _experts(xs, block_expert, n_used, layer, w_gate, w_up, w_down):
    _, _, dm, de = w_gate.shape
    tile = dm // LANES
    rows = MOE_BLOCK * tile
    grid_spec = pltpu.PrefetchScalarGridSpec(
        num_scalar_prefetch=2, grid=(xs.shape[0] // rows,),
        in_specs=[pl.BlockSpec((rows, LANES), lambda j, be, nu: (jnp.minimum(j, nu[0] - 1), 0)),
                  pl.BlockSpec((None, None, dm, de), lambda j, be, nu: (layer, be[j], 0, 0)),
                  pl.BlockSpec((None, None, dm, de), lambda j, be, nu: (layer, be[j], 0, 0)),
                  pl.BlockSpec((None, None, de, dm), lambda j, be, nu: (layer, be[j], 0, 0))],
        out_specs=pl.BlockSpec((rows, LANES), lambda j, be, nu: (j, 0)),
        scratch_shapes=[pltpu.VMEM((dm, de), BF16), pltpu.VMEM((dm, de), BF16), pltpu.VMEM((de, dm), BF16)],
    )
    return pl.pallas_call(
        _expert_kernel, grid_spec=grid_spec,
        out_shape=jax.ShapeDtypeStruct(xs.shape, F32),
        compiler_params=_params(("arbitrary",)), name="expert_blocks",
    )(block_expert, n_used, xs, w_gate, w_up, w_down)


def _moe(h_tiles, meta, cnt, layer, w_gate, w_up, w_down):
    n = meta.shape[0]
    n_exp = w_gate.shape[1]
    blk = MOE_BLOCK
    n_blocks = -(-2 * n // blk) + n_exp
    tile_cnt = cnt.reshape(-1, SUBLANES, LANES)[:, 0, :].astype(I32)
    counts = jnp.sum(tile_cnt, axis=0)
    padded = (counts + blk - 1) // blk * blk
    pad_ends = jnp.cumsum(padded)
    pad_starts = pad_ends - padded
    before = jnp.cumsum(tile_cnt, axis=0) - tile_cnt
    base = jnp.repeat((pad_starts[None, :] + before).astype(F32), SUBLANES, axis=0)
    fill_lo = (pad_starts + counts)[:n_exp]
    fill_hi = pad_ends[:n_exp]
    n_used = pad_ends[n_exp - 1] // blk
    block_start = jnp.minimum(jnp.arange(n_blocks, dtype=I32), n_used - 1) * blk
    block_expert = jnp.sum((pad_ends[None, :n_exp] <= block_start[:, None]).astype(I32), axis=1)
    block_expert = jnp.minimum(block_expert, n_exp - 1)
    n_used = n_used.astype(I32).reshape(1)
    xs, slots = _dispatch(h_tiles, meta, base, fill_lo, fill_hi, n_used, n_blocks * blk)
    ys = _experts(xs, block_expert, n_used, layer, w_gate, w_up, w_down)
    return ys, slots


def kernel(x, g_mix, w_in, conv_w, conv_b, conv_ln_g, conv_ln_b, ssm_lam_re, ssm_lam_im, ssm_log_dt,
           ssm_b_re, ssm_b_im, ssm_c_re, ssm_c_im, ssm_d, ssm_glu_w, ssm_glu_b, g_branch, w_out, g_ffn,
           w_router_group, b_router_group, w_router_expert, b_router_expert, w_gate, w_up, w_down, g_final):
    bsz, seq, dm = x.shape
    depth = w_in.shape[0]
    cw = conv_w.shape[-1]
    assert bsz % SUBLANES == 0
    row2 = lambda v: v.reshape(1, -1)

    xt = _to_time_major(x)
    moe = None
    for l in range(depth):
        w_in_bf = w_in[l].astype(BF16)
        if moe is None:
            u, s_in = _in_projection(xt, row2(g_mix[l]), w_in_bf, cw)
        else:
            xt, u, s_in = _combine_in_projection(xt, *moe, row2(g_mix[l]), w_in_bf, cw)
        gb = g_branch[l]
        yc = _conv_branch(u, jnp.repeat(conv_w[l], SUBLANES, axis=0), row2(conv_b[l]), row2(conv_ln_g[l]),
                          row2(conv_ln_b[l]), row2(gb[:cw]), bsz)
        wd, a_re, a_im, wc = _ssm_weights(ssm_lam_re[l], ssm_lam_im[l], ssm_log_dt[l],
                                          ssm_b_re[l], ssm_b_im[l], ssm_c_re[l], ssm_c_im[l])
        ys = _ssm_branch(s_in, wd, a_re, a_im, wc, row2(ssm_d[l]), ssm_glu_w[l].astype(BF16),
                         row2(ssm_glu_b[l]), row2(gb[cw:]), bsz)
        w_r = jnp.concatenate([w_router_group[l], w_router_expert[l]], axis=1)
        w_r = jnp.pad(w_r, ((0, 0), (0, LANES - w_r.shape[1])))
        wr_hi = w_r.astype(BF16)
        wr_lo = (w_r - wr_hi.astype(F32)).astype(BF16)
        b_r = jnp.concatenate([b_router_group[l], b_router_expert[l]])
        b_r = jnp.pad(b_r, (0, LANES - b_r.shape[0])).reshape(1, LANES)
        xt, h, meta, cnt = _outproj_router(xt, yc, ys, w_out[l].astype(BF16), row2(g_ffn[l]), wr_hi, wr_lo, b_r)
        y_sorted, slots = _moe(h, meta, cnt, l, w_gate, w_up, w_down)
        moe = (meta, slots, y_sorted)
    out_tm = _combine_final_norm(xt, *moe, row2(g_final))
    return _from_time_major(out_tm, bsz)
```

```python
import functools

import jax
import jax.numpy as jnp
from jax import lax
from jax.experimental import pallas as pl
from jax.experimental.pallas import tpu as pltpu

F32 = jnp.float32
BF16 = jnp.bfloat16
I32 = jnp.int32

EPS = 1e-6
CONV_KERNEL = 31
SSM_GROUP = 16
SSM_STATE = 64
MOE_GROUPS = 4
EXPERTS_PER_GROUP = 8
N_EXPERTS = MOE_GROUPS * EXPERTS_PER_GROUP
LANES = 128
SUBLANES = 8
MXU_COLS = 256
VMEM_LIMIT = 56 * 1024 * 1024

ROW_TILE = 512
HALO_STEPS = 32
MOE_BLOCK = 512
LAYOUT_STEPS = 256


def _params(sem, **kw):
    return pltpu.CompilerParams(dimension_semantics=sem, vmem_limit_bytes=VMEM_LIMIT, **kw)


def _rms(x, g):
    return x * lax.rsqrt(jnp.mean(x * x, axis=-1, keepdims=True) + EPS) * g


def _to_tm_kernel(x_ref, o_ref):
    bsz, tt, _ = x_ref.shape
    for b in range(bsz):
        o_ref[pl.ds(b, tt, stride=bsz), :] = x_ref[b]


def _from_tm_kernel(x_ref, o_ref):
    bsz, tt, _ = o_ref.shape
    for b in range(bsz):
        o_ref[b] = x_ref[pl.ds(b, tt, stride=bsz), :]


def _to_time_major(x):
    bsz, seq, dm = x.shape
    tt = min(seq, LAYOUT_STEPS)
    return pl.pallas_call(
        _to_tm_kernel, grid=(seq // tt, dm // LANES),
        in_specs=[pl.BlockSpec((bsz, tt, LANES), lambda i, c: (0, i, c))],
        out_specs=pl.BlockSpec((tt * bsz, LANES), lambda i, c: (i, c)),
        out_shape=jax.ShapeDtypeStruct((seq * bsz, dm), x.dtype),
        compiler_params=_params(("parallel", "parallel")), name="to_time_major",
    )(x)


def _from_time_major(xt, bsz):
    n, dm = xt.shape
    seq = n // bsz
    tt = min(seq, LAYOUT_STEPS)
    return pl.pallas_call(
        _from_tm_kernel, grid=(seq // tt, dm // LANES),
        in_specs=[pl.BlockSpec((tt * bsz, LANES), lambda i, c: (i, c))],
        out_specs=pl.BlockSpec((bsz, tt, LANES), lambda i, c: (0, i, c)),
        out_shape=jax.ShapeDtypeStruct((bsz, seq, dm), xt.dtype),
        compiler_params=_params(("parallel", "parallel")), name="from_time_major",
    )(xt)


def _inproj_body(x, g_ref, w_ref, u_ref, s_ref):
    cw = u_ref.shape[-1]
    h = _rms(x, g_ref[...]).astype(BF16)
    z = jnp.dot(h, w_ref[...], preferred_element_type=F32)
    u_ref[...] = z[:, :cw] * jax.nn.sigmoid(z[:, cw:2 * cw])
    s_ref[...] = z[:, 2 * cw:]


def _inproj_kernel(x_ref, g_ref, w_ref, u_ref, s_ref):
    _inproj_body(x_ref[...], g_ref, w_ref, u_ref, s_ref)


def _in_projection(x, g, w_bf, conv_width):
    n, dm = x.shape
    tm = min(ROW_TILE, n)
    in_w = w_bf.shape[1]
    sw = in_w - 2 * conv_width
    row = lambda i: (i, 0)
    fixed = lambda i: (0, 0)
    return pl.pallas_call(
        _inproj_kernel, grid=(n // tm,),
        in_specs=[pl.BlockSpec((tm, dm), row), pl.BlockSpec((1, dm), fixed), pl.BlockSpec((dm, in_w), fixed)],
        out_specs=[pl.BlockSpec((tm, conv_width), row), pl.BlockSpec((tm, sw), row)],
        out_shape=[jax.ShapeDtypeStruct((n, conv_width), F32), jax.ShapeDtypeStruct((n, sw), F32)],
        compiler_params=_params(("parallel",)), name="in_projection",
    )(x, g, w_bf)


def _token_rows(ref, first, n_rows, lead=()):
    return jnp.concatenate(
        [ref[lead + (pl.ds(first * SUBLANES + s, n_rows, stride=SUBLANES), slice(None))] for s in range(SUBLANES)],
        axis=-1)


def _store_token_rows(ref, val):
    n_rows = val.shape[0]
    for s in range(SUBLANES):
        ref[pl.ds(s, n_rows, stride=SUBLANES), :] = val[:, s * LANES:(s + 1) * LANES]


def _gather_expert_rows(i, n_steps, tm, slots_hbm, ys_hbm, sl_smem, ybuf, csem, gsem):
    def fetch(tile, b):
        cp = pltpu.make_async_copy(slots_hbm.at[tile], sl_smem.at[b], csem)
        cp.start()
        cp.wait()

        def issue(r, carry):
            for k in range(2):
                src = pl.multiple_of(sl_smem[b, k, r] * SUBLANES, SUBLANES)
                dst = pl.multiple_of((k * tm + r) * SUBLANES, SUBLANES)
                pltpu.make_async_copy(ys_hbm.at[pl.ds(src, SUBLANES)], ybuf.at[b].at[pl.ds(dst, SUBLANES)],
                                      gsem.at[b]).start()
            return carry

        lax.fori_loop(0, tm, issue, 0, unroll=8)

    @pl.when(i == 0)
    def _():
        fetch(0, 0)

    @pl.when(i + 1 < n_steps)
    def _():
        fetch(i + 1, (i + 1) % 2)

    b = i % 2
    pltpu.make_async_copy(ys_hbm.at[pl.ds(0, 2 * tm * SUBLANES)], ybuf.at[b], gsem.at[b]).wait()
    return b


def _combined_residual(x_ref, meta_ref, ybuf, b):
    tm = x_ref.shape[0]
    meta = meta_ref[...]
    y0 = _token_rows(ybuf, 0, tm, lead=(b,))
    y1 = _token_rows(ybuf, tm, tm, lead=(b,))
    return x_ref[...] + meta[:, 2:3] * y0 + meta[:, 3:4] * y1


def _combine_inproj_kernel(x_ref, meta_ref, slots_hbm, ys_hbm, g_ref, w_ref, xo_ref, u_ref, s_ref,
                           sl_smem, ybuf, csem, gsem):
    i = pl.program_id(0)
    b = _gather_expert_rows(i, pl.num_programs(0), x_ref.shape[0], slots_hbm, ys_hbm, sl_smem, ybuf, csem, gsem)
    x = _combined_residual(x_ref, meta_ref, ybuf, b)
    xo_ref[...] = x
    _inproj_body(x, g_ref, w_ref, u_ref, s_ref)


def _combine_final_kernel(x_ref, meta_ref, slots_hbm, ys_hbm, g_ref, o_ref, sl_smem, ybuf, csem, gsem):
    i = pl.program_id(0)
    b = _gather_expert_rows(i, pl.num_programs(0), x_ref.shape[0], slots_hbm, ys_hbm, sl_smem, ybuf, csem, gsem)
    o_ref[...] = _rms(_combined_residual(x_ref, meta_ref, ybuf, b), g_ref[...])


def _combine_scratch(tm, dm):
    tile_rows = dm // LANES
    assert tile_rows == SUBLANES
    return [pltpu.SMEM((2, SUBLANES, tm), I32), pltpu.VMEM((2, 2 * tm * tile_rows, LANES), F32),
            pltpu.SemaphoreType.DMA, pltpu.SemaphoreType.DMA((2,))]


def _combine_in_projection(x, meta, slots, ys, g, w_bf, conv_width):
    n, dm = x.shape
    tm = slots.shape[-1]
    in_w = w_bf.shape[1]
    sw = in_w - 2 * conv_width
    row = lambda i: (i, 0)
    fixed = lambda i: (0, 0)
    hbm = pl.BlockSpec(memory_space=pl.ANY)
    return pl.pallas_call(
        _combine_inproj_kernel, grid=(n // tm,),
        in_specs=[pl.BlockSpec((tm, dm), row), pl.BlockSpec((tm, LANES), row), hbm, hbm,
                  pl.BlockSpec((1, dm), fixed), pl.BlockSpec((dm, in_w), fixed)],
        out_specs=[pl.BlockSpec((tm, dm), row), pl.BlockSpec((tm, conv_width), row), pl.BlockSpec((tm, sw), row)],
        out_shape=[jax.ShapeDtypeStruct((n, dm), F32), jax.ShapeDtypeStruct((n, conv_width), F32),
                   jax.ShapeDtypeStruct((n, sw), F32)],
        scratch_shapes=_combine_scratch(tm, dm),
        compiler_params=_params(("arbitrary",), disable_bounds_checks=True), name="combine_in_projection",
    )(x, meta, slots, ys, g, w_bf)


def _combine_final_norm(x, meta, slots, ys, g):
    n, dm = x.shape
    tm = slots.shape[-1]
    row = lambda i: (i, 0)
    hbm = pl.BlockSpec(memory_space=pl.ANY)
    return pl.pallas_call(
        _combine_final_kernel, grid=(n // tm,),
        in_specs=[pl.BlockSpec((tm, dm), row), pl.BlockSpec((tm, LANES), row), hbm, hbm,
                  pl.BlockSpec((1, dm), lambda i: (0, 0))],
        out_specs=pl.BlockSpec((tm, dm), row),
        out_shape=jax.ShapeDtypeStruct((n, dm), F32),
        scratch_shapes=_combine_scratch(tm, dm),
        compiler_params=_params(("arbitrary",), disable_bounds_checks=True), name="combine_final_norm",
    )(x, meta, slots, ys, g)


CONV_CHUNK = 32
CONV_UNROLL = 2


def _conv_kernel(halo_ref, u_ref, w_ref, b_ref, lng_ref, lnb_ref, gb_ref, o_ref, buf_ref, *, bsz):
    rows = u_ref.shape[0]
    halo = halo_ref.shape[0]
    pieces = CONV_CHUNK // SUBLANES
    i = pl.program_id(0)

    @pl.when(i == 0)
    def _():
        buf_ref[0:halo, :] = jnp.zeros(halo_ref.shape, F32)

    @pl.when(i > 0)
    def _():
        buf_ref[0:halo, :] = halo_ref[...]

    buf_ref[halo:, :] = u_ref[...]

    def chunk(c, carry):
        r0 = pl.multiple_of(c * CONV_CHUNK, CONV_CHUNK)
        first = r0 + halo - (CONV_KERNEL - 1) * bsz

        def tap(k, acc):
            wk = w_ref[pl.ds(pl.multiple_of(k * SUBLANES, SUBLANES), SUBLANES), :]
            xk = buf_ref[pl.ds(pl.multiple_of(first + k * bsz, SUBLANES), CONV_CHUNK), :]
            return acc + xk * jnp.concatenate([wk] * pieces, axis=0)

        acc = lax.fori_loop(0, CONV_KERNEL, tap, jnp.zeros((CONV_CHUNK, u_ref.shape[1]), F32), unroll=CONV_UNROLL)
        o_ref[pl.ds(r0, CONV_CHUNK), :] = acc
        return carry

    lax.fori_loop(0, rows // CONV_CHUNK, chunk, 0)

    y = o_ref[...] + b_ref[...]
    mu = jnp.mean(y, axis=-1, keepdims=True)
    yc = y - mu
    yn = yc * lax.rsqrt(jnp.mean(yc * yc, axis=-1, keepdims=True) + EPS)
    yn = yn * lng_ref[...] + lnb_ref[...]
    act = yn * jax.nn.sigmoid(yn)
    o_ref[...] = _rms(act, gb_ref[...])


def _conv_branch(u, w_rep, b, ln_g, ln_b, gb, bsz):
    n, cw = u.shape
    halo = HALO_STEPS * bsz
    rows = min(ROW_TILE, n)
    assert rows % halo == 0 and halo >= (CONV_KERNEL - 1) * bsz
    ratio = rows // halo
    fixed = lambda i: (0, 0)
    return pl.pallas_call(
        functools.partial(_conv_kernel, bsz=bsz),
        grid=(n // rows,),
        in_specs=[pl.BlockSpec((halo, cw), lambda i: (jnp.maximum(i * ratio - 1, 0), 0)),
                  pl.BlockSpec((rows, cw), lambda i: (i, 0)),
                  pl.BlockSpec((CONV_KERNEL * SUBLANES, cw), fixed),
                  pl.BlockSpec((1, cw), fixed), pl.BlockSpec((1, cw), fixed),
                  pl.BlockSpec((1, cw), fixed), pl.BlockSpec((1, cw), fixed)],
        out_specs=pl.BlockSpec((rows, cw), lambda i: (i, 0)),
        out_shape=jax.ShapeDtypeStruct((n, cw), F32),
        scratch_shapes=[pltpu.VMEM((halo + rows, cw), F32)],
        compiler_params=_params(("parallel",)), name="conv_branch",
    )(u, u, w_rep, b, ln_g, ln_b, gb)


SCAN_TILES = 4
SCAN_UNROLL = 2


def _ssm_kernel(s_ref, wd_ref, are_ref, aim_ref, wc_ref, dvec_ref, gw_ref, gbias_ref, gb_ref,
                o_ref, d_ref, hb_ref, h_ref, *, bsz):
    rows, sw = s_ref.shape
    n_state = d_ref.shape[1]
    n_tiles = n_state // MXU_COLS
    tiles_per_lane_block = LANES // (2 * SSM_GROUP)
    halves = bsz // SUBLANES
    steps = rows // bsz

    @pl.when(pl.program_id(0) == 0)
    def _():
        h_ref[...] = jnp.zeros(h_ref.shape, F32)

    s = s_ref[...]
    s_bf = s.astype(BF16)
    for j in range(n_tiles):
        q = j // tiles_per_lane_block
        d_ref[:, j * MXU_COLS:(j + 1) * MXU_COLS] = jnp.dot(
            s_bf[:, q * LANES:(q + 1) * LANES], wd_ref[j], preferred_element_type=F32)

    for jq in range(n_tiles // SCAN_TILES):
        cols = [(jq * SCAN_TILES + jj) * MXU_COLS for jj in range(SCAN_TILES)]
        init = []
        for c in cols:
            for hf in range(halves):
                r = hf * SUBLANES
                init.append(h_ref[r:r + SUBLANES, c:c + LANES])
                init.append(h_ref[r:r + SUBLANES, c + LANES:c + 2 * LANES])

        def step(t, carry, cols=cols):
            row0 = pl.multiple_of(t * bsz, bsz)
            new = []
            k = 0
            for c in cols:
                a_re = jnp.broadcast_to(are_ref[:, c // 2:c // 2 + LANES], (SUBLANES, LANES))
                a_im = jnp.broadcast_to(aim_ref[:, c // 2:c // 2 + LANES], (SUBLANES, LANES))
                res_re, res_im = [], []
                for hf in range(halves):
                    r0 = row0 + hf * SUBLANES
                    hr, hi = carry[k], carry[k + 1]
                    k += 2
                    dr = d_ref[pl.ds(r0, SUBLANES), c:c + LANES]
                    di = d_ref[pl.ds(r0, SUBLANES), c + LANES:c + 2 * LANES]
                    nr = a_re * hr - a_im * hi + dr
                    ni = a_re * hi + a_im * hr + di
                    new += [nr, ni]
                    res_re.append(nr)
                    res_im.append(ni)
                hb_ref[pl.ds(row0, bsz), c:c + LANES] = jnp.concatenate(res_re, axis=0).astype(BF16)
                hb_ref[pl.ds(row0, bsz), c + LANES:c + 2 * LANES] = jnp.concatenate(res_im, axis=0).astype(BF16)
            return tuple(new)

        fin = lax.fori_loop(0, steps, step, tuple(init), unroll=SCAN_UNROLL)
        k = 0
        for c in cols:
            for hf in range(halves):
                r = hf * SUBLANES
                h_ref[r:r + SUBLANES, c:c + LANES] = fin[k]
                h_ref[r:r + SUBLANES, c + LANES:c + 2 * LANES] = fin[k + 1]
                k += 2

    n_out = sw // LANES
    kw = n_state // n_out
    ys = [jnp.dot(hb_ref[:, q * kw:(q + 1) * kw], wc_ref[q], preferred_element_type=F32) for q in range(n_out)]
    y = jnp.concatenate(ys, axis=-1) + dvec_ref[...] * s
    act = jax.nn.gelu(y)
    z = jnp.dot(act.astype(BF16), gw_ref[...], preferred_element_type=F32) + gbias_ref[...]
    o_ref[...] = _rms(act * jax.nn.sigmoid(z), gb_ref[...])


def _ssm_weights(lam_re, lam_im, log_dt, b_re, b_im, c_re, c_im):
    g, n = lam_re.shape
    dt = jnp.exp(log_dt)[:, None]
    mag = jnp.exp(lam_re * dt)
    ab_re = mag * jnp.cos(lam_im * dt)
    ab_im = mag * jnp.sin(lam_im * dt)
    den = lam_re * lam_re + lam_im * lam_im
    nr = ab_re - 1.0
    q_re = (nr * lam_re + ab_im * lam_im) / den
    q_im = (ab_im * lam_re - nr * lam_im) / den
    bb_re = q_re[..., None] * b_re - q_im[..., None] * b_im
    bb_im = q_re[..., None] * b_im + q_im[..., None] * b_re
    sel = jnp.eye(g, dtype=F32).reshape(g, g // 2, 2)
    bbp = jnp.stack([bb_re, bb_im])
    wd = jnp.einsum('gji,pgnc->gcjpin', sel, bbp).reshape(g * SSM_GROUP, 2 * g * n)
    ccp = jnp.stack([c_re, -c_im])
    wc = jnp.einsum('gji,pgcn->jpingc', sel, ccp).reshape(2 * g * n, g * SSM_GROUP)
    n_tiles = 2 * g * n // MXU_COLS
    per_block = LANES // (2 * SSM_GROUP)
    wd_t = jnp.stack([wd[(j // per_block) * LANES:(j // per_block + 1) * LANES,
                         j * MXU_COLS:(j + 1) * MXU_COLS] for j in range(n_tiles)])
    n_out = g * SSM_GROUP // LANES
    kw = 2 * g * n // n_out
    wc_t = jnp.stack([wc[q * kw:(q + 1) * kw, q * LANES:(q + 1) * LANES] for q in range(n_out)])
    a_re = ab_re.reshape(1, g * n)
    a_im = ab_im.reshape(1, g * n)
    return wd_t.astype(BF16), a_re, a_im, wc_t.astype(BF16)


def _ssm_branch(s_in, wd, a_re, a_im, wc, dvec, glu_w_bf, glu_b, gb, bsz):
    n, sw = s_in.shape
    rows = min(ROW_TILE, n)
    n_state = 2 * a_re.shape[1]
    fixed2 = lambda i: (0, 0)
    fixed3 = lambda i: (0, 0, 0)
    vec = pl.BlockSpec((1, sw), fixed2)
    return pl.pallas_call(
        functools.partial(_ssm_kernel, bsz=bsz),
        grid=(n // rows,),
        in_specs=[pl.BlockSpec((rows, sw), lambda i: (i, 0)),
                  pl.BlockSpec(wd.shape, fixed3),
                  pl.BlockSpec(a_re.shape, fixed2), pl.BlockSpec(a_im.shape, fixed2),
                  pl.BlockSpec(wc.shape, fixed3),
                  vec, pl.BlockSpec((sw, sw), fixed2), vec, vec],
        out_specs=pl.BlockSpec((rows, sw), lambda i: (i, 0)),
        out_shape=jax.ShapeDtypeStruct((n, sw), F32),
        scratch_shapes=[pltpu.VMEM((rows, n_state), F32), pltpu.VMEM((rows, n_state), BF16),
                        pltpu.VMEM((bsz, n_state), F32)],
        compiler_params=_params(("arbitrary",)), name="ssm_branch",
    )(s_in, wd, a_re, a_im, wc, dvec, glu_w_bf, glu_b, gb)


def _outproj_router_kernel(x_ref, yc_ref, ys_ref, wo_ref, g_ref, wr_hi_ref, wr_lo_ref, br_ref,
                           xo_ref, h_ref, meta_ref, cnt_ref):
    ycat = jnp.concatenate([yc_ref[...].astype(BF16), ys_ref[...].astype(BF16)], axis=-1)
    x = x_ref[...] + jnp.dot(ycat, wo_ref[...], preferred_element_type=F32)
    xo_ref[...] = x
    h = _rms(x, g_ref[...])
    _store_token_rows(h_ref, h)
    h_hi = h.astype(BF16)
    h_lo = (h - h_hi.astype(F32)).astype(BF16)
    w_hi = wr_hi_ref[...]
    lg = (jnp.dot(h_hi, w_hi, preferred_element_type=F32)
          + jnp.dot(h_lo, w_hi, preferred_element_type=F32)
          + jnp.dot(h_hi, wr_lo_ref[...], preferred_element_type=F32)) + br_ref[...]
    lane = lax.broadcasted_iota(I32, lg.shape, 1).astype(F32)
    ninf = jnp.float32(-jnp.inf)
    big = jnp.float32(LANES)
    is_g = lane < MOE_GROUPS
    gl = jnp.where(is_g, lg, ninf)
    gmax = jnp.max(gl, axis=-1, keepdims=True)
    grp = jnp.min(jnp.where(gl == gmax, lane, big), axis=-1, keepdims=True)
    den = jnp.sum(jnp.where(is_g, jnp.exp(gl - gmax), 0.0), axis=-1, keepdims=True)
    p_grp = 1.0 / den
    lo = MOE_GROUPS + EXPERTS_PER_GROUP * grp
    el = jnp.where((lane >= lo) & (lane < lo + EXPERTS_PER_GROUP), lg, ninf)
    v1 = jnp.max(el, axis=-1, keepdims=True)
    i1 = jnp.min(jnp.where(el == v1, lane, big), axis=-1, keepdims=True)
    el2 = jnp.where(lane == i1, ninf, el)
    v2 = jnp.max(el2, axis=-1, keepdims=True)
    i2 = jnp.min(jnp.where(el2 == v2, lane, big), axis=-1, keepdims=True)
    t = jnp.exp(v2 - v1)
    g1 = p_grp / (1.0 + t)
    g2 = p_grp * t / (1.0 + t)
    e1 = i1 - MOE_GROUPS
    e2 = i2 - MOE_GROUPS
    meta_ref[...] = jnp.where(lane == 0, e1, jnp.where(lane == 1, e2,
                                                       jnp.where(lane == 2, g1, jnp.where(lane == 3, g2, 0.0))))
    hits = jnp.where(lane == e1, 1.0, 0.0) + jnp.where(lane == e2, 1.0, 0.0)
    cnt_ref[...] = jnp.broadcast_to(jnp.sum(hits, axis=0, keepdims=True), cnt_ref.shape)


def _outproj_router(x, yc, ys, wo_bf, g, wr_hi, wr_lo, br):
    n, dm = x.shape
    cw = yc.shape[1]
    sw = ys.shape[1]
    tm = min(ROW_TILE, n)
    row = lambda i: (i, 0)
    fixed = lambda i: (0, 0)
    return pl.pallas_call(
        _outproj_router_kernel, grid=(n // tm,),
        in_specs=[pl.BlockSpec((tm, dm), row), pl.BlockSpec((tm, cw), row), pl.BlockSpec((tm, sw), row),
                  pl.BlockSpec((cw + sw, dm), fixed), pl.BlockSpec((1, dm), fixed),
                  pl.BlockSpec((dm, LANES), fixed), pl.BlockSpec((dm, LANES), fixed),
                  pl.BlockSpec((1, LANES), fixed)],
        out_specs=[pl.BlockSpec((tm, dm), row), pl.BlockSpec((tm * dm // LANES, LANES), row),
                   pl.BlockSpec((tm, LANES), row), pl.BlockSpec((SUBLANES, LANES), row)],
        out_shape=[jax.ShapeDtypeStruct((n, dm), F32), jax.ShapeDtypeStruct((n * dm // LANES, LANES), F32),
                   jax.ShapeDtypeStruct((n, LANES), F32),
                   jax.ShapeDtypeStruct((n // tm * SUBLANES, LANES), F32)],
        compiler_params=_params(("parallel",)), name="out_projection_router",
    )(x, yc, ys, wo_bf, g, wr_hi, wr_lo, br)


def _dispatch_kernel(fill_lo_ref, fill_hi_ref, nused_ref, meta_ref, base_ref, tril_ref, h_ref, xs_hbm, slots_ref,
                     sl_vmem, sl_smem, zbuf, csem, dsem, zsem):
    i = pl.program_id(0)
    n_steps = pl.num_programs(0)
    tm = meta_ref.shape[0]
    tile = SUBLANES
    meta = meta_ref[...]
    lane = lax.broadcasted_iota(I32, meta.shape, 1).astype(F32)
    oh0 = lane == meta[:, 0:1]
    oh1 = lane == meta[:, 1:2]
    hits = (jnp.where(oh0, 1.0, 0.0) + jnp.where(oh1, 1.0, 0.0)).astype(BF16)
    earlier = jnp.dot(tril_ref[...], hits, preferred_element_type=F32)
    pos = base_ref[0:1, :] + earlier
    s0 = jnp.sum(jnp.where(oh0, pos, 0.0), axis=-1, keepdims=True)
    s1 = jnp.sum(jnp.where(oh1, pos, 0.0), axis=-1, keepdims=True)
    slot_cols = jnp.where(lane == 0, s0, jnp.where(lane == 1, s1, 0.0))
    slot_rows = jnp.transpose(slot_cols)[0:SUBLANES, :].astype(I32)
    slots_ref[...] = slot_rows
    sl_vmem[...] = slot_rows
    cp = pltpu.make_async_copy(sl_vmem, sl_smem, csem)
    cp.start()
    cp.wait()

    def issue(r, carry):
        src = pl.multiple_of(r * tile, tile)
        for k in range(2):
            dst = pl.multiple_of(sl_smem[k, r] * tile, tile)
            pltpu.make_async_copy(h_ref.at[pl.ds(src, tile)], xs_hbm.at[pl.ds(dst, tile)], dsem).start()
        return carry

    lax.fori_loop(0, tm, issue, 0, unroll=8)
    pltpu.make_async_copy(h_ref, xs_hbm.at[pl.ds(0, tm * tile)], dsem).wait()
    pltpu.make_async_copy(h_ref, xs_hbm.at[pl.ds(0, tm * tile)], dsem).wait()

    @pl.when(i == n_steps - 1)
    def _():
        zbuf[...] = jnp.zeros(zbuf.shape, F32)
        n_exp = fill_lo_ref.shape[0]

        def each_pad_row(fn):
            def per_expert(e, carry):
                def per_row(r, c2):
                    fn(pl.multiple_of(r * tile, tile))
                    return c2
                lax.fori_loop(fill_lo_ref[e], fill_hi_ref[e], per_row, 0)
                return carry
            lax.fori_loop(0, n_exp, per_expert, 0)

        each_pad_row(lambda r: pltpu.make_async_copy(
            zbuf.at[pl.ds(0, tile)], xs_hbm.at[pl.ds(r, tile)], zsem).start())
        each_pad_row(lambda r: pltpu.make_async_copy(
            zbuf.at[pl.ds(0, tile)], xs_hbm.at[pl.ds(0, tile)], zsem).wait())

        blk_rows = zbuf.shape[0]

        def each_spare_block(fn):
            def per_block(j, carry):
                fn(pl.multiple_of(j * blk_rows, blk_rows))
                return carry
            lax.fori_loop(nused_ref[0], xs_hbm.shape[0] // blk_rows, per_block, 0)

        each_spare_block(lambda r: pltpu.make_async_copy(zbuf, xs_hbm.at[pl.ds(r, blk_rows)], zsem).start())
        each_spare_block(lambda r: pltpu.make_async_copy(zbuf, xs_hbm.at[pl.ds(0, blk_rows)], zsem).wait())


def _dispatch(h_tiles, meta, base, fill_lo, fill_hi, n_used, cap):
    n = meta.shape[0]
    tile = h_tiles.shape[0] // n
    tm = min(ROW_TILE, n)
    n_tiles = n // tm
    tril = jnp.tril(jnp.ones((tm, tm), F32), -1).astype(BF16)
    hbm = pl.BlockSpec(memory_space=pl.ANY)
    grid_spec = pltpu.PrefetchScalarGridSpec(
        num_scalar_prefetch=3, grid=(n_tiles,),
        in_specs=[pl.BlockSpec((tm, LANES), lambda i, lo, hi, nu: (i, 0)),
                  pl.BlockSpec((SUBLANES, LANES), lambda i, lo, hi, nu: (i, 0)),
                  pl.BlockSpec((tm, tm), lambda i, lo, hi, nu: (0, 0)),
                  pl.BlockSpec((tm * tile, LANES), lambda i, lo, hi, nu: (i, 0))],
        out_specs=[hbm, pl.BlockSpec((None, SUBLANES, tm), lambda i, lo, hi, nu: (i, 0, 0))],
        scratch_shapes=[pltpu.VMEM((SUBLANES, tm), I32), pltpu.SMEM((SUBLANES, tm), I32),
                        pltpu.VMEM((MOE_BLOCK * tile, LANES), F32),
                        pltpu.SemaphoreType.DMA, pltpu.SemaphoreType.DMA, pltpu.SemaphoreType.DMA],
    )
    return pl.pallas_call(
        _dispatch_kernel, grid_spec=grid_spec,
        out_shape=[jax.ShapeDtypeStruct((cap * tile, LANES), F32),
                   jax.ShapeDtypeStruct((n_tiles, SUBLANES, tm), I32)],
        compiler_params=_params(("arbitrary",), disable_bounds_checks=True), name="dispatch",
    )(fill_lo, fill_hi, n_used, meta, base, tril, h_tiles)


def _expert_kernel(be_ref, nused_ref, xs_ref, wg_ref, wu_ref, wd_ref, ys_ref, wg_bf, wu_bf, wd_bf):
    j = pl.program_id(0)
    blk = xs_ref.shape[0] // SUBLANES

    @pl.when((j == 0) | (be_ref[j] != be_ref[jnp.maximum(j - 1, 0)]))
    def _():
        wg_bf[...] = wg_ref[...].astype(BF16)
        wu_bf[...] = wu_ref[...].astype(BF16)
        wd_bf[...] = wd_ref[...].astype(BF16)

    @pl.when(j < nused_ref[0])
    def _():
        x = _token_rows(xs_ref, 0, blk).astype(BF16)
        gate = jnp.dot(x, wg_bf[...], preferred_element_type=F32)
        up = jnp.dot(x, wu_bf[...], preferred_element_type=F32)
        hid = gate * jax.nn.sigmoid(gate) * up
        _store_token_rows(ys_ref, jnp.dot(hid.astype(BF16), wd_bf[...], preferred_element_type=F32))

    @pl.when(j >= nused_ref[0])
    def _():
        ys_ref[...] = jnp.zeros(ys_ref.shape, F32)


def _experts(xs, block_expert, n_used, layer, w_gate, w_up, w_down):
    _, _, dm, de = w_gate.shape
    tile = dm // LANES
    rows = MOE_BLOCK * tile
    grid_spec = pltpu.PrefetchScalarGridSpec(
        num_scalar_prefetch=2, grid=(xs.shape[0] // rows,),
        in_specs=[pl.BlockSpec((rows, LANES), lambda j, be, nu: (jnp.minimum(j, nu[0] - 1), 0)),
                  pl.BlockSpec((None, None, dm, de), lambda j, be, nu: (layer, be[j], 0, 0)),
                  pl.BlockSpec((None, None, dm, de), lambda j, be, nu: (layer, be[j], 0, 0)),
                  pl.BlockSpec((None, None, de, dm), lambda j, be, nu: (layer, be[j], 0, 0))],
        out_specs=pl.BlockSpec((rows, LANES), lambda j, be, nu: (j, 0)),
        scratch_shapes=[pltpu.VMEM((dm, de), BF16), pltpu.VMEM((dm, de), BF16), pltpu.VMEM((de, dm), BF16)],
    )
    return pl.pallas_call(
        _expert_kernel, grid_spec=grid_spec,
        out_shape=jax.ShapeDtypeStruct(xs.shape, F32),
        compiler_params=_params(("arbitrary",)), name="expert_blocks",
    )(block_expert, n_used, xs, w_gate, w_up, w_down)


def _moe(h_tiles, meta, cnt, layer, w_gate, w_up, w_down):
    n = meta.shape[0]
    n_exp = w_gate.shape[1]
    blk = MOE_BLOCK
    n_blocks = -(-2 * n // blk) + n_exp
    tile_cnt = cnt.reshape(-1, SUBLANES, LANES)[:, 0, :].astype(I32)
    counts = jnp.sum(tile_cnt, axis=0)
    padded = (counts + blk - 1) // blk * blk
    pad_ends = jnp.cumsum(padded)
    pad_starts = pad_ends - padded
    before = jnp.cumsum(tile_cnt, axis=0) - tile_cnt
    base = jnp.repeat((pad_starts[None, :] + before).astype(F32), SUBLANES, axis=0)
    fill_lo = (pad_starts + counts)[:n_exp]
    fill_hi = pad_ends[:n_exp]
    n_used = pad_ends[n_exp - 1] // blk
    block_start = jnp.minimum(jnp.arange(n_blocks, dtype=I32), n_used - 1) * blk
    block_expert = jnp.sum((pad_ends[None, :n_exp] <= block_start[:, None]).astype(I32), axis=1)
    block_expert = jnp.minimum(block_expert, n_exp - 1)
    n_used = n_used.astype(I32).reshape(1)
    xs, slots = _dispatch(h_tiles, meta, base, fill_lo, fill_hi, n_used, n_blocks * blk)
    ys = _experts(xs, block_expert, n_used, layer, w_gate, w_up, w_down)
    return ys, slots


def kernel(x, g_mix, w_in, conv_w, conv_b, conv_ln_g, conv_ln_b, ssm_lam_re, ssm_lam_im, ssm_log_dt,
           ssm_b_re, ssm_b_im, ssm_c_re, ssm_c_im, ssm_d, ssm_glu_w, ssm_glu_b, g_branch, w_out, g_ffn,
           w_router_group, b_router_group, w_router_expert, b_router_expert, w_gate, w_up, w_down, g_final):
    bsz, seq, dm = x.shape
    depth = w_in.shape[0]
    cw = conv_w.shape[-1]
    assert bsz % SUBLANES == 0
    row2 = lambda v: v.reshape(1, -1)

    xt = _to_time_major(x)
    moe = None
    for l in range(depth):
        w_in_bf = w_in[l].astype(BF16)
        if moe is None:
            u, s_in = _in_projection(xt, row2(g_mix[l]), w_in_bf, cw)
        else:
            xt, u, s_in = _combine_in_projection(xt, *moe, row2(g_mix[l]), w_in_bf, cw)
        gb = g_branch[l]
        yc = _conv_branch(u, jnp.repeat(conv_w[l], SUBLANES, axis=0), row2(conv_b[l]), row2(conv_ln_g[l]),
                          row2(conv_ln_b[l]), row2(gb[:cw]), bsz)
        wd, a_re, a_im, wc = _ssm_weights(ssm_lam_re[l], ssm_lam_im[l], ssm_log_dt[l],
                                          ssm_b_re[l], ssm_b_im[l], ssm_c_re[l], ssm_c_im[l])
        ys = _ssm_branch(s_in, wd, a_re, a_im, wc, row2(ssm_d[l]), ssm_glu_w[l].astype(BF16),
                         row2(ssm_glu_b[l]), row2(gb[cw:]), bsz)
        w_r = jnp.concatenate([w_router_group[l], w_router_expert[l]], axis=1)
        w_r = jnp.pad(w_r, ((0, 0), (0, LANES - w_r.shape[1])))
        wr_hi = w_r.astype(BF16)
        wr_lo = (w_r - wr_hi.astype(F32)).astype(BF16)
        b_r = jnp.concatenate([b_router_group[l], b_router_expert[l]])
        b_r = jnp.pad(b_r, (0, LANES - b_r.shape[0])).reshape(1, LANES)
        xt, h, meta, cnt = _outproj_router(xt, yc, ys, w_out[l].astype(BF16), row2(g_ffn[l]), wr_hi, wr_lo, b_r)
        y_sorted, slots = _moe(h, meta, cnt, l, w_gate, w_up, w_down)
        moe = (meta, slots, y_sorted)
    out_tm = _combine_final_norm(xt, *moe, row2(g_final))
    return _from_time_major(out_tm, bsz)
```

```python
import functools

import jax
import jax.numpy as jnp
from jax import lax
from jax.experimental import pallas as pl
from jax.experimental.pallas import tpu as pltpu

F32 = jnp.float32
BF16 = jnp.bfloat16
I32 = jnp.int32

EPS = 1e-6
CONV_KERNEL = 31
SSM_GROUP = 16
SSM_STATE = 64
MOE_GROUPS = 4
EXPERTS_PER_GROUP = 8
N_EXPERTS = MOE_GROUPS * EXPERTS_PER_GROUP
LANES = 128
SUBLANES = 8
MXU_COLS = 256
VMEM_LIMIT = 56 * 1024 * 1024

ROW_TILE = 512
HALO_STEPS = 32
MOE_BLOCK = 512
LAYOUT_STEPS = 256


def _params(sem, **kw):
    return pltpu.CompilerParams(dimension_semantics=sem, vmem_limit_bytes=VMEM_LIMIT, **kw)


def _rms(x, g):
    return x * lax.rsqrt(jnp.mean(x * x, axis=-1, keepdims=True) + EPS) * g


def _to_tm_kernel(x_ref, o_ref):
    bsz, tt, _ = x_ref.shape
    for b in range(bsz):
        o_ref[pl.ds(b, tt, stride=bsz), :] = x_ref[b]


def _from_tm_kernel(x_ref, o_ref):
    bsz, tt, _ = o_ref.shape
    for b in range(bsz):
        o_ref[b] = x_ref[pl.ds(b, tt, stride=bsz), :]


def _to_time_major(x):
    bsz, seq, dm = x.shape
    tt = min(seq, LAYOUT_STEPS)
    return pl.pallas_call(
        _to_tm_kernel, grid=(seq // tt, dm // LANES),
        in_specs=[pl.BlockSpec((bsz, tt, LANES), lambda i, c: (0, i, c))],
        out_specs=pl.BlockSpec((tt * bsz, LANES), lambda i, c: (i, c)),
        out_shape=jax.ShapeDtypeStruct((seq * bsz, dm), x.dtype),
        compiler_params=_params(("parallel", "parallel")), name="to_time_major",
    )(x)


def _from_time_major(xt, bsz):
    n, dm = xt.shape
    seq = n // bsz
    tt = min(seq, LAYOUT_STEPS)
    return pl.pallas_call(
        _from_tm_kernel, grid=(seq // tt, dm // LANES),
        in_specs=[pl.BlockSpec((tt * bsz, LANES), lambda i, c: (i, c))],
        out_specs=pl.BlockSpec((bsz, tt, LANES), lambda i, c: (0, i, c)),
        out_shape=jax.ShapeDtypeStruct((bsz, seq, dm), xt.dtype),
        compiler_params=_params(("parallel", "parallel")), name="from_time_major",
    )(xt)


def _inproj_body(x, g_ref, w_ref, u_ref, s_ref):
    cw = u_ref.shape[-1]
    h = _rms(x, g_ref[...]).astype(BF16)
    z = jnp.dot(h, w_ref[...], preferred_element_type=F32)
    u_ref[...] = z[:, :cw] * jax.nn.sigmoid(z[:, cw:2 * cw])
    s_ref[...] = z[:, 2 * cw:]


def _inproj_kernel(x_ref, g_ref, w_ref, u_ref, s_ref):
    _inproj_body(x_ref[...], g_ref, w_ref, u_ref, s_ref)


def _in_projection(x, g, w_bf, conv_width):
    n, dm = x.shape
    tm = min(ROW_TILE, n)
    in_w = w_bf.shape[1]
    sw = in_w - 2 * conv_width
    row = lambda i: (i, 0)
    fixed = lambda i: (0, 0)
    return pl.pallas_call(
        _inproj_kernel, grid=(n // tm,),
        in_specs=[pl.BlockSpec((tm, dm), row), pl.BlockSpec((1, dm), fixed), pl.BlockSpec((dm, in_w), fixed)],
        out_specs=[pl.BlockSpec((tm, conv_width), row), pl.BlockSpec((tm, sw), row)],
        out_shape=[jax.ShapeDtypeStruct((n, conv_width), F32), jax.ShapeDtypeStruct((n, sw), F32)],
        compiler_params=_params(("parallel",)), name="in_projection",
    )(x, g, w_bf)


def _token_rows(ref, first, n_rows, lead=()):
    return jnp.concatenate(
        [ref[lead + (pl.ds(first * SUBLANES + s, n_rows, stride=SUBLANES), slice(None))] for s in range(SUBLANES)],
        axis=-1)


def _store_token_rows(ref, val):
    n_rows = val.shape[0]
    for s in range(SUBLANES):
        ref[pl.ds(s, n_rows, stride=SUBLANES), :] = val[:, s * LANES:(s + 1) * LANES]


def _gather_expert_rows(i, n_steps, tm, slots_hbm, ys_hbm, sl_smem, ybuf, csem, gsem):
    def fetch(tile, b):
        cp = pltpu.make_async_copy(slots_hbm.at[tile], sl_smem.at[b], csem)
        cp.start()
        cp.wait()

        def issue(r, carry):
            for k in range(2):
                src = pl.multiple_of(sl_smem[b, k, r] * SUBLANES, SUBLANES)
                dst = pl.multiple_of((k * tm + r) * SUBLANES, SUBLANES)
                pltpu.make_async_copy(ys_hbm.at[pl.ds(src, SUBLANES)], ybuf.at[b].at[pl.ds(dst, SUBLANES)],
                                      gsem.at[b]).start(priority=k)
            return carry

        lax.fori_loop(0, tm, issue, 0, unroll=8)

    @pl.when(i == 0)
    def _():
        fetch(0, 0)

    @pl.when(i + 1 < n_steps)
    def _():
        fetch(i + 1, (i + 1) % 2)

    b = i % 2
    pltpu.make_async_copy(ys_hbm.at[pl.ds(0, 2 * tm * SUBLANES)], ybuf.at[b], gsem.at[b]).wait()
    return b


def _combined_residual(x_ref, meta_ref, ybuf, b):
    tm = x_ref.shape[0]
    meta = meta_ref[...]
    y0 = _token_rows(ybuf, 0, tm, lead=(b,))
    y1 = _token_rows(ybuf, tm, tm, lead=(b,))
    return x_ref[...] + meta[:, 2:3] * y0 + meta[:, 3:4] * y1


def _combine_inproj_kernel(x_ref, meta_ref, slots_hbm, ys_hbm, g_ref, w_ref, xo_ref, u_ref, s_ref,
                           sl_smem, ybuf, csem, gsem):
    i = pl.program_id(0)
    b = _gather_expert_rows(i, pl.num_programs(0), x_ref.shape[0], slots_hbm, ys_hbm, sl_smem, ybuf, csem, gsem)
    x = _combined_residual(x_ref, meta_ref, ybuf, b)
    xo_ref[...] = x
    _inproj_body(x, g_ref, w_ref, u_ref, s_ref)


def _combine_final_kernel(x_ref, meta_ref, slots_hbm, ys_hbm, g_ref, o_ref, sl_smem, ybuf, csem, gsem):
    i = pl.program_id(0)
    b = _gather_expert_rows(i, pl.num_programs(0), x_ref.shape[0], slots_hbm, ys_hbm, sl_smem, ybuf, csem, gsem)
    o_ref[...] = _rms(_combined_residual(x_ref, meta_ref, ybuf, b), g_ref[...])


def _combine_scratch(tm, dm):
    tile_rows = dm // LANES
    assert tile_rows == SUBLANES
    return [pltpu.SMEM((2, SUBLANES, tm), I32), pltpu.VMEM((2, 2 * tm * tile_rows, LANES), F32),
            pltpu.SemaphoreType.DMA, pltpu.SemaphoreType.DMA((2,))]


def _combine_in_projection(x, meta, slots, ys, g, w_bf, conv_width):
    n, dm = x.shape
    tm = slots.shape[-1]
    in_w = w_bf.shape[1]
    sw = in_w - 2 * conv_width
    row = lambda i: (i, 0)
    fixed = lambda i: (0, 0)
    hbm = pl.BlockSpec(memory_space=pl.ANY)
    return pl.pallas_call(
        _combine_inproj_kernel, grid=(n // tm,),
        in_specs=[pl.BlockSpec((tm, dm), row), pl.BlockSpec((tm, LANES), row), hbm, hbm,
                  pl.BlockSpec((1, dm), fixed), pl.BlockSpec((dm, in_w), fixed)],
        out_specs=[pl.BlockSpec((tm, dm), row), pl.BlockSpec((tm, conv_width), row), pl.BlockSpec((tm, sw), row)],
        out_shape=[jax.ShapeDtypeStruct((n, dm), F32), jax.ShapeDtypeStruct((n, conv_width), F32),
                   jax.ShapeDtypeStruct((n, sw), F32)],
        scratch_shapes=_combine_scratch(tm, dm),
        compiler_params=_params(("arbitrary",), disable_bounds_checks=True), name="combine_in_projection",
    )(x, meta, slots, ys, g, w_bf)


def _combine_final_norm(x, meta, slots, ys, g):
    n, dm = x.shape
    tm = slots.shape[-1]
    row = lambda i: (i, 0)
    hbm = pl.BlockSpec(memory_space=pl.ANY)
    return pl.pallas_call(
        _combine_final_kernel, grid=(n // tm,),
        in_specs=[pl.BlockSpec((tm, dm), row), pl.BlockSpec((tm, LANES), row), hbm, hbm,
                  pl.BlockSpec((1, dm), lambda i: (0, 0))],
        out_specs=pl.BlockSpec((tm, dm), row),
        out_shape=jax.ShapeDtypeStruct((n, dm), F32),
        scratch_shapes=_combine_scratch(tm, dm),
        compiler_params=_params(("arbitrary",), disable_bounds_checks=True), name="combine_final_norm",
    )(x, meta, slots, ys, g)


CONV_CHUNK = 32
CONV_UNROLL = 2


def _conv_kernel(halo_ref, u_ref, w_ref, b_ref, lng_ref, lnb_ref, gb_ref, o_ref, buf_ref, *, bsz):
    rows = u_ref.shape[0]
    halo = halo_ref.shape[0]
    pieces = CONV_CHUNK // SUBLANES
    i = pl.program_id(0)

    @pl.when(i == 0)
    def _():
        buf_ref[0:halo, :] = jnp.zeros(halo_ref.shape, F32)

    @pl.when(i > 0)
    def _():
        buf_ref[0:halo, :] = halo_ref[...]

    buf_ref[halo:, :] = u_ref[...]

    def chunk(c, carry):
        r0 = pl.multiple_of(c * CONV_CHUNK, CONV_CHUNK)
        first = r0 + halo - (CONV_KERNEL - 1) * bsz

        def tap(k, acc):
            wk = w_ref[pl.ds(pl.multiple_of(k * SUBLANES, SUBLANES), SUBLANES), :]
            xk = buf_ref[pl.ds(pl.multiple_of(first + k * bsz, SUBLANES), CONV_CHUNK), :]
            return acc + xk * jnp.concatenate([wk] * pieces, axis=0)

        acc = lax.fori_loop(0, CONV_KERNEL, tap, jnp.zeros((CONV_CHUNK, u_ref.shape[1]), F32), unroll=CONV_UNROLL)
        o_ref[pl.ds(r0, CONV_CHUNK), :] = acc
        return carry

    lax.fori_loop(0, rows // CONV_CHUNK, chunk, 0)

    y = o_ref[...] + b_ref[...]
    mu = jnp.mean(y, axis=-1, keepdims=True)
    yc = y - mu
    yn = yc * lax.rsqrt(jnp.mean(yc * yc, axis=-1, keepdims=True) + EPS)
    yn = yn * lng_ref[...] + lnb_ref[...]
    act = yn * jax.nn.sigmoid(yn)
    o_ref[...] = _rms(act, gb_ref[...])


def _conv_branch(u, w_rep, b, ln_g, ln_b, gb, bsz):
    n, cw = u.shape
    halo = HALO_STEPS * bsz
    rows = min(ROW_TILE, n)
    assert rows % halo == 0 and halo >= (CONV_KERNEL - 1) * bsz
    ratio = rows // halo
    fixed = lambda i: (0, 0)
    return pl.pallas_call(
        functools.partial(_conv_kernel, bsz=bsz),
        grid=(n // rows,),
        in_specs=[pl.BlockSpec((halo, cw), lambda i: (jnp.maximum(i * ratio - 1, 0), 0)),
                  pl.BlockSpec((rows, cw), lambda i: (i, 0)),
                  pl.BlockSpec((CONV_KERNEL * SUBLANES, cw), fixed),
                  pl.BlockSpec((1, cw), fixed), pl.BlockSpec((1, cw), fixed),
                  pl.BlockSpec((1, cw), fixed), pl.BlockSpec((1, cw), fixed)],
        out_specs=pl.BlockSpec((rows, cw), lambda i: (i, 0)),
        out_shape=jax.ShapeDtypeStruct((n, cw), F32),
        scratch_shapes=[pltpu.VMEM((halo + rows, cw), F32)],
        compiler_params=_params(("parallel",)), name="conv_branch",
    )(u, u, w_rep, b, ln_g, ln_b, gb)


SCAN_TILES = 4
SCAN_UNROLL = 2


def _ssm_kernel(s_ref, wd_ref, are_ref, aim_ref, wc_ref, dvec_ref, gw_ref, gbias_ref, gb_ref,
                o_ref, d_ref, hb_ref, h_ref, *, bsz):
    rows, sw = s_ref.shape
    n_state = d_ref.shape[1]
    n_tiles = n_state // MXU_COLS
    tiles_per_lane_block = LANES // (2 * SSM_GROUP)
    halves = bsz // SUBLANES
    steps = rows // bsz

    @pl.when(pl.program_id(0) == 0)
    def _():
        h_ref[...] = jnp.zeros(h_ref.shape, F32)

    s = s_ref[...]
    s_bf = s.astype(BF16)
    for j in range(n_tiles):
        q = j // tiles_per_lane_block
        d_ref[:, j * MXU_COLS:(j + 1) * MXU_COLS] = jnp.dot(
            s_bf[:, q * LANES:(q + 1) * LANES], wd_ref[j], preferred_element_type=F32)

    for jq in range(n_tiles // SCAN_TILES):
        cols = [(jq * SCAN_TILES + jj) * MXU_COLS for jj in range(SCAN_TILES)]
        init = []
        for c in cols:
            for hf in range(halves):
                r = hf * SUBLANES
                init.append(h_ref[r:r + SUBLANES, c:c + LANES])
                init.append(h_ref[r:r + SUBLANES, c + LANES:c + 2 * LANES])

        def step(t, carry, cols=cols):
            row0 = pl.multiple_of(t * bsz, bsz)
            new = []
            k = 0
            for c in cols:
                a_re = jnp.broadcast_to(are_ref[:, c // 2:c // 2 + LANES], (SUBLANES, LANES))
                a_im = jnp.broadcast_to(aim_ref[:, c // 2:c // 2 + LANES], (SUBLANES, LANES))
                res_re, res_im = [], []
                for hf in range(halves):
                    r0 = row0 + hf * SUBLANES
                    hr, hi = carry[k], carry[k + 1]
                    k += 2
                    dr = d_ref[pl.ds(r0, SUBLANES), c:c + LANES]
                    di = d_ref[pl.ds(r0, SUBLANES), c + LANES:c + 2 * LANES]
                    nr = a_re * hr - a_im * hi + dr
                    ni = a_re * hi + a_im * hr + di
                    new += [nr, ni]
                    res_re.append(nr)
                    res_im.append(ni)
                hb_ref[pl.ds(row0, bsz), c:c + LANES] = jnp.concatenate(res_re, axis=0).astype(BF16)
                hb_ref[pl.ds(row0, bsz), c + LANES:c + 2 * LANES] = jnp.concatenate(res_im, axis=0).astype(BF16)
            return tuple(new)

        fin = lax.fori_loop(0, steps, step, tuple(init), unroll=SCAN_UNROLL)
        k = 0
        for c in cols:
            for hf in range(halves):
                r = hf * SUBLANES
                h_ref[r:r + SUBLANES, c:c + LANES] = fin[k]
                h_ref[r:r + SUBLANES, c + LANES:c + 2 * LANES] = fin[k + 1]
                k += 2

    n_out = sw // LANES
    kw = n_state // n_out
    ys = [jnp.dot(hb_ref[:, q * kw:(q + 1) * kw], wc_ref[q], preferred_element_type=F32) for q in range(n_out)]
    y = jnp.concatenate(ys, axis=-1) + dvec_ref[...] * s
    act = jax.nn.gelu(y)
    z = jnp.dot(act.astype(BF16), gw_ref[...], preferred_element_type=F32) + gbias_ref[...]
    o_ref[...] = _rms(act * jax.nn.sigmoid(z), gb_ref[...])


def _ssm_weights(lam_re, lam_im, log_dt, b_re, b_im, c_re, c_im):
    g, n = lam_re.shape
    dt = jnp.exp(log_dt)[:, None]
    mag = jnp.exp(lam_re * dt)
    ab_re = mag * jnp.cos(lam_im * dt)
    ab_im = mag * jnp.sin(lam_im * dt)
    den = lam_re * lam_re + lam_im * lam_im
    nr = ab_re - 1.0
    q_re = (nr * lam_re + ab_im * lam_im) / den
    q_im = (ab_im * lam_re - nr * lam_im) / den
    bb_re = q_re[..., None] * b_re - q_im[..., None] * b_im
    bb_im = q_re[..., None] * b_im + q_im[..., None] * b_re
    sel = jnp.eye(g, dtype=F32).reshape(g, g // 2, 2)
    bbp = jnp.stack([bb_re, bb_im])
    wd = jnp.einsum('gji,pgnc->gcjpin', sel, bbp).reshape(g * SSM_GROUP, 2 * g * n)
    ccp = jnp.stack([c_re, -c_im])
    wc = jnp.einsum('gji,pgcn->jpingc', sel, ccp).reshape(2 * g * n, g * SSM_GROUP)
    n_tiles = 2 * g * n // MXU_COLS
    per_block = LANES // (2 * SSM_GROUP)
    wd_t = jnp.stack([wd[(j // per_block) * LANES:(j // per_block + 1) * LANES,
                         j * MXU_COLS:(j + 1) * MXU_COLS] for j in range(n_tiles)])
    n_out = g * SSM_GROUP // LANES
    kw = 2 * g * n // n_out
    wc_t = jnp.stack([wc[q * kw:(q + 1) * kw, q * LANES:(q + 1) * LANES] for q in range(n_out)])
    a_re = ab_re.reshape(1, g * n)
    a_im = ab_im.reshape(1, g * n)
    return wd_t.astype(BF16), a_re, a_im, wc_t.astype(BF16)


def _ssm_branch(s_in, wd, a_re, a_im, wc, dvec, glu_w_bf, glu_b, gb, bsz):
    n, sw = s_in.shape
    rows = min(ROW_TILE, n)
    n_state = 2 * a_re.shape[1]
    fixed2 = lambda i: (0, 0)
    fixed3 = lambda i: (0, 0, 0)
    vec = pl.BlockSpec((1, sw), fixed2)
    return pl.pallas_call(
        functools.partial(_ssm_kernel, bsz=bsz),
        grid=(n // rows,),
        in_specs=[pl.BlockSpec((rows, sw), lambda i: (i, 0)),
                  pl.BlockSpec(wd.shape, fixed3),
                  pl.BlockSpec(a_re.shape, fixed2), pl.BlockSpec(a_im.shape, fixed2),
                  pl.BlockSpec(wc.shape, fixed3),
                  vec, pl.BlockSpec((sw, sw), fixed2), vec, vec],
        out_specs=pl.BlockSpec((rows, sw), lambda i: (i, 0)),
        out_shape=jax.ShapeDtypeStruct((n, sw), F32),
        scratch_shapes=[pltpu.VMEM((rows, n_state), F32), pltpu.VMEM((rows, n_state), BF16),
                        pltpu.VMEM((bsz, n_state), F32)],
        compiler_params=_params(("arbitrary",)), name="ssm_branch",
    )(s_in, wd, a_re, a_im, wc, dvec, glu_w_bf, glu_b, gb)


def _outproj_router_kernel(x_ref, yc_ref, ys_ref, wo_ref, g_ref, wr_hi_ref, wr_lo_ref, br_ref,
                           xo_ref, h_ref, meta_ref, cnt_ref):
    ycat = jnp.concatenate([yc_ref[...].astype(BF16), ys_ref[...].astype(BF16)], axis=-1)
    x = x_ref[...] + jnp.dot(ycat, wo_ref[...], preferred_element_type=F32)
    xo_ref[...] = x
    h = _rms(x, g_ref[...])
    _store_token_rows(h_ref, h)
    h_hi = h.astype(BF16)
    h_lo = (h - h_hi.astype(F32)).astype(BF16)
    w_hi = wr_hi_ref[...]
    lg = (jnp.dot(h_hi, w_hi, preferred_element_type=F32)
          + jnp.dot(h_lo, w_hi, preferred_element_type=F32)
          + jnp.dot(h_hi, wr_lo_ref[...], preferred_element_type=F32)) + br_ref[...]
    lane = lax.broadcasted_iota(I32, lg.shape, 1).astype(F32)
    ninf = jnp.float32(-jnp.inf)
    big = jnp.float32(LANES)
    is_g = lane < MOE_GROUPS
    gl = jnp.where(is_g, lg, ninf)
    gmax = jnp.max(gl, axis=-1, keepdims=True)
    grp = jnp.min(jnp.where(gl == gmax, lane, big), axis=-1, keepdims=True)
    den = jnp.sum(jnp.where(is_g, jnp.exp(gl - gmax), 0.0), axis=-1, keepdims=True)
    p_grp = 1.0 / den
    lo = MOE_GROUPS + EXPERTS_PER_GROUP * grp
    el = jnp.where((lane >= lo) & (lane < lo + EXPERTS_PER_GROUP), lg, ninf)
    v1 = jnp.max(el, axis=-1, keepdims=True)
    i1 = jnp.min(jnp.where(el == v1, lane, big), axis=-1, keepdims=True)
    el2 = jnp.where(lane == i1, ninf, el)
    v2 = jnp.max(el2, axis=-1, keepdims=True)
    i2 = jnp.min(jnp.where(el2 == v2, lane, big), axis=-1, keepdims=True)
    t = jnp.exp(v2 - v1)
    g1 = p_grp / (1.0 + t)
    g2 = p_grp * t / (1.0 + t)
    e1 = i1 - MOE_GROUPS
    e2 = i2 - MOE_GROUPS
    meta_ref[...] = jnp.where(lane == 0, e1, jnp.where(lane == 1, e2,
                                                       jnp.where(lane == 2, g1, jnp.where(lane == 3, g2, 0.0))))
    hits = jnp.where(lane == e1, 1.0, 0.0) + jnp.where(lane == e2, 1.0, 0.0)
    cnt_ref[...] = jnp.broadcast_to(jnp.sum(hits, axis=0, keepdims=True), cnt_ref.shape)


def _outproj_router(x, yc, ys, wo_bf, g, wr_hi, wr_lo, br):
    n, dm = x.shape
    cw = yc.shape[1]
    sw = ys.shape[1]
    tm = min(ROW_TILE, n)
    row = lambda i: (i, 0)
    fixed = lambda i: (0, 0)
    return pl.pallas_call(
        _outproj_router_kernel, grid=(n // tm,),
        in_specs=[pl.BlockSpec((tm, dm), row), pl.BlockSpec((tm, cw), row), pl.BlockSpec((tm, sw), row),
                  pl.BlockSpec((cw + sw, dm), fixed), pl.BlockSpec((1, dm), fixed),
                  pl.BlockSpec((dm, LANES), fixed), pl.BlockSpec((dm, LANES), fixed),
                  pl.BlockSpec((1, LANES), fixed)],
        out_specs=[pl.BlockSpec((tm, dm), row), pl.BlockSpec((tm * dm // LANES, LANES), row),
                   pl.BlockSpec((tm, LANES), row), pl.BlockSpec((SUBLANES, LANES), row)],
        out_shape=[jax.ShapeDtypeStruct((n, dm), F32), jax.ShapeDtypeStruct((n * dm // LANES, LANES), F32),
                   jax.ShapeDtypeStruct((n, LANES), F32),
                   jax.ShapeDtypeStruct((n // tm * SUBLANES, LANES), F32)],
        compiler_params=_params(("parallel",)), name="out_projection_router",
    )(x, yc, ys, wo_bf, g, wr_hi, wr_lo, br)


def _dispatch_kernel(fill_lo_ref, fill_hi_ref, nused_ref, meta_ref, base_ref, tril_ref, h_ref, xs_hbm, slots_ref,
                     sl_vmem, sl_smem, zbuf, csem, dsem, zsem):
    i = pl.program_id(0)
    n_steps = pl.num_programs(0)
    tm = meta_ref.shape[0]
    tile = SUBLANES
    meta = meta_ref[...]
    lane = lax.broadcasted_iota(I32, meta.shape, 1).astype(F32)
    oh0 = lane == meta[:, 0:1]
    oh1 = lane == meta[:, 1:2]
    hits = (jnp.where(oh0, 1.0, 0.0) + jnp.where(oh1, 1.0, 0.0)).astype(BF16)
    earlier = jnp.dot(tril_ref[...], hits, preferred_element_type=F32)
    pos = base_ref[0:1, :] + earlier
    s0 = jnp.sum(jnp.where(oh0, pos, 0.0), axis=-1, keepdims=True)
    s1 = jnp.sum(jnp.where(oh1, pos, 0.0), axis=-1, keepdims=True)
    slot_cols = jnp.where(lane == 0, s0, jnp.where(lane == 1, s1, 0.0))
    slot_rows = jnp.transpose(slot_cols)[0:SUBLANES, :].astype(I32)
    slots_ref[...] = slot_rows
    sl_vmem[...] = slot_rows
    cp = pltpu.make_async_copy(sl_vmem, sl_smem, csem)
    cp.start()
    cp.wait()

    def issue(r, carry):
        src = pl.multiple_of(r * tile, tile)
        for k in range(2):
            dst = pl.multiple_of(sl_smem[k, r] * tile, tile)
            pltpu.make_async_copy(h_ref.at[pl.ds(src, tile)], xs_hbm.at[pl.ds(dst, tile)],
                                  dsem).start(priority=k)
        return carry

    lax.fori_loop(0, tm, issue, 0, unroll=8)
    pltpu.make_async_copy(h_ref, xs_hbm.at[pl.ds(0, tm * tile)], dsem).wait()
    pltpu.make_async_copy(h_ref, xs_hbm.at[pl.ds(0, tm * tile)], dsem).wait()

    @pl.when(i == n_steps - 1)
    def _():
        zbuf[...] = jnp.zeros(zbuf.shape, F32)
        n_exp = fill_lo_ref.shape[0]

        def each_pad_row(fn):
            def per_expert(e, carry):
                def per_row(r, c2):
                    fn(pl.multiple_of(r * tile, tile))
                    return c2
                lax.fori_loop(fill_lo_ref[e], fill_hi_ref[e], per_row, 0)
                return carry
            lax.fori_loop(0, n_exp, per_expert, 0)

        each_pad_row(lambda r: pltpu.make_async_copy(
            zbuf.at[pl.ds(0, tile)], xs_hbm.at[pl.ds(r, tile)], zsem).start())
        each_pad_row(lambda r: pltpu.make_async_copy(
            zbuf.at[pl.ds(0, tile)], xs_hbm.at[pl.ds(0, tile)], zsem).wait())

        blk_rows = zbuf.shape[0]

        def each_spare_block(fn):
            def per_block(j, carry):
                fn(pl.multiple_of(j * blk_rows, blk_rows))
                return carry
            lax.fori_loop(nused_ref[0], xs_hbm.shape[0] // blk_rows, per_block, 0)

        each_spare_block(lambda r: pltpu.make_async_copy(zbuf, xs_hbm.at[pl.ds(r, blk_rows)], zsem).start())
        each_spare_block(lambda r: pltpu.make_async_copy(zbuf, xs_hbm.at[pl.ds(0, blk_rows)], zsem).wait())


def _dispatch(h_tiles, meta, base, fill_lo, fill_hi, n_used, cap):
    n = meta.shape[0]
    tile = h_tiles.shape[0] // n
    tm = min(ROW_TILE, n)
    n_tiles = n // tm
    tril = jnp.tril(jnp.ones((tm, tm), F32), -1).astype(BF16)
    hbm = pl.BlockSpec(memory_space=pl.ANY)
    grid_spec = pltpu.PrefetchScalarGridSpec(
        num_scalar_prefetch=3, grid=(n_tiles,),
        in_specs=[pl.BlockSpec((tm, LANES), lambda i, lo, hi, nu: (i, 0)),
                  pl.BlockSpec((SUBLANES, LANES), lambda i, lo, hi, nu: (i, 0)),
                  pl.BlockSpec((tm, tm), lambda i, lo, hi, nu: (0, 0)),
                  pl.BlockSpec((tm * tile, LANES), lambda i, lo, hi, nu: (i, 0))],
        out_specs=[hbm, pl.BlockSpec((None, SUBLANES, tm), lambda i, lo, hi, nu: (i, 0, 0))],
        scratch_shapes=[pltpu.VMEM((SUBLANES, tm), I32), pltpu.SMEM((SUBLANES, tm), I32),
                        pltpu.VMEM((MOE_BLOCK * tile, LANES), F32),
                        pltpu.SemaphoreType.DMA, pltpu.SemaphoreType.DMA, pltpu.SemaphoreType.DMA],
    )
    return pl.pallas_call(
        _dispatch_kernel, grid_spec=grid_spec,
        out_shape=[jax.ShapeDtypeStruct((cap * tile, LANES), F32),
                   jax.ShapeDtypeStruct((n_tiles, SUBLANES, tm), I32)],
        compiler_params=_params(("arbitrary",), disable_bounds_checks=True), name="dispatch",
    )(fill_lo, fill_hi, n_used, meta, base, tril, h_tiles)


def _expert_kernel(be_ref, nused_ref, xs_ref, wg_ref, wu_ref, wd_ref, ys_ref, wg_bf, wu_bf, wd_bf):
    j = pl.program_id(0)
    blk = xs_ref.shape[0] // SUBLANES

    @pl.when((j == 0) | (be_ref[j] != be_ref[jnp.maximum(j - 1, 0)]))
    def _():
        wg_bf[...] = wg_ref[...].astype(BF16)
        wu_bf[...] = wu_ref[...].astype(BF16)
        wd_bf[...] = wd_ref[...].astype(BF16)

    @pl.when(j < nused_ref[0])
    def _():
        x = _token_rows(xs_ref, 0, blk).astype(BF16)
        gate = jnp.dot(x, wg_bf[...], preferred_element_type=F32)
        up = jnp.dot(x, wu_bf[...], preferred_element_type=F32)
        hid = gate * jax.nn.sigmoid(gate) * up
        _store_token_rows(ys_ref, jnp.dot(hid.astype(BF16), wd_bf[...], preferred_element_type=F32))

    @pl.when(j >= nused_ref[0])
    def _():
        ys_ref[...] = jnp.zeros(ys_ref.shape, F32)


def _experts(xs, block_expert, n_used, layer, w_gate, w_up, w_down):
    _, _, dm, de = w_gate.shape
    tile = dm // LANES
    rows = MOE_BLOCK * tile
    grid_spec = pltpu.PrefetchScalarGridSpec(
        num_scalar_prefetch=2, grid=(xs.shape[0] // rows,),
        in_specs=[pl.BlockSpec((rows, LANES), lambda j, be, nu: (jnp.minimum(j, nu[0] - 1), 0)),
                  pl.BlockSpec((None, None, dm, de), lambda j, be, nu: (layer, be[j], 0, 0)),
                  pl.BlockSpec((None, None, dm, de), lambda j, be, nu: (layer, be[j], 0, 0)),
                  pl.BlockSpec((None, None, de, dm), lambda j, be, nu: (layer, be[j], 0, 0))],
        out_specs=pl.BlockSpec((rows, LANES), lambda j, be, nu: (j, 0)),
        scratch_shapes=[pltpu.VMEM((dm, de), BF16), pltpu.VMEM((dm, de), BF16), pltpu.VMEM((de, dm), BF16)],
    )
    return pl.pallas_call(
        _expert_kernel, grid_spec=grid_spec,
        out_shape=jax.ShapeDtypeStruct(xs.shape, F32),
        compiler_params=_params(("arbitrary",)), name="expert_blocks",
    )(block_expert, n_used, xs, w_gate, w_up, w_down)


def _moe(h_tiles, meta, cnt, layer, w_gate, w_up, w_down):
    n = meta.shape[0]
    n_exp = w_gate.shape[1]
    blk = MOE_BLOCK
    n_blocks = -(-2 * n // blk) + n_exp
    tile_cnt = cnt.reshape(-1, SUBLANES, LANES)[:, 0, :].astype(I32)
    counts = jnp.sum(tile_cnt, axis=0)
    padded = (counts + blk - 1) // blk * blk
    pad_ends = jnp.cumsum(padded)
    pad_starts = pad_ends - padded
    before = jnp.cumsum(tile_cnt, axis=0) - tile_cnt
    base = jnp.repeat((pad_starts[None, :] + before).astype(F32), SUBLANES, axis=0)
    fill_lo = (pad_starts + counts)[:n_exp]
    fill_hi = pad_ends[:n_exp]
    n_used = pad_ends[n_exp - 1] // blk
    block_start = jnp.minimum(jnp.arange(n_blocks, dtype=I32), n_used - 1) * blk
    block_expert = jnp.sum((pad_ends[None, :n_exp] <= block_start[:, None]).astype(I32), axis=1)
    block_expert = jnp.minimum(block_expert, n_exp - 1)
    n_used = n_used.astype(I32).reshape(1)
    xs, slots = _dispatch(h_tiles, meta, base, fill_lo, fill_hi, n_used, n_blocks * blk)
    ys = _experts(xs, block_expert, n_used, layer, w_gate, w_up, w_down)
    return ys, slots


def kernel(x, g_mix, w_in, conv_w, conv_b, conv_ln_g, conv_ln_b, ssm_lam_re, ssm_lam_im, ssm_log_dt,
           ssm_b_re, ssm_b_im, ssm_c_re, ssm_c_im, ssm_d, ssm_glu_w, ssm_glu_b, g_branch, w_out, g_ffn,
           w_router_group, b_router_group, w_router_expert, b_router_expert, w_gate, w_up, w_down, g_final):
    bsz, seq, dm = x.shape
    depth = w_in.shape[0]
    cw = conv_w.shape[-1]
    assert bsz % SUBLANES == 0
    row2 = lambda v: v.reshape(1, -1)

    xt = _to_time_major(x)
    moe = None
    for l in range(depth):
        w_in_bf = w_in[l].astype(BF16)
        if moe is None:
            u, s_in = _in_projection(xt, row2(g_mix[l]), w_in_bf, cw)
        else:
            xt, u, s_in = _combine_in_projection(xt, *moe, row2(g_mix[l]), w_in_bf, cw)
        gb = g_branch[l]
        yc = _conv_branch(u, jnp.repeat(conv_w[l], SUBLANES, axis=0), row2(conv_b[l]), row2(conv_ln_g[l]),
                          row2(conv_ln_b[l]), row2(gb[:cw]), bsz)
        wd, a_re, a_im, wc = _ssm_weights(ssm_lam_re[l], ssm_lam_im[l], ssm_log_dt[l],
                                          ssm_b_re[l], ssm_b_im[l], ssm_c_re[l], ssm_c_im[l])
        ys = _ssm_branch(s_in, wd, a_re, a_im, wc, row2(ssm_d[l]), ssm_glu_w[l].astype(BF16),
                         row2(ssm_glu_b[l]), row2(gb[cw:]), bsz)
        w_r = jnp.concatenate([w_router_group[l], w_router_expert[l]], axis=1)
        w_r = jnp.pad(w_r, ((0, 0), (0, LANES - w_r.shape[1])))
        wr_hi = w_r.astype(BF16)
        wr_lo = (w_r - wr_hi.astype(F32)).astype(BF16)
        b_r = jnp.concatenate([b_router_group[l], b_router_expert[l]])
        b_r = jnp.pad(b_r, (0, LANES - b_r.shape[0])).reshape(1, LANES)
        xt, h, meta, cnt = _outproj_router(xt, yc, ys, w_out[l].astype(BF16), row2(g_ffn[l]), wr_hi, wr_lo, b_r)
        y_sorted, slots = _moe(h, meta, cnt, l, w_gate, w_up, w_down)
        moe = (meta, slots, y_sorted)
    out_tm = _combine_final_norm(xt, *moe, row2(g_final))
    return _from_time_major(out_tm, bsz)
```
